```python
import math
import jax, jax.numpy as jnp
from jax import lax
import numpy as np

D_MODEL = 2048
BATCH = 2
SEQ = 4096
DEPTH = 1

HEAD_DIM = 128
N_HEADS_SB = (D_MODEL // HEAD_DIM) // 2
N_HEADS_DSA = (D_MODEL // HEAD_DIM) - N_HEADS_SB
SB_WIDTH = N_HEADS_SB * HEAD_DIM
DSA_WIDTH = N_HEADS_DSA * HEAD_DIM
IDX_HEADS = 16
IDX_DIM = 64
TOPK_MAX = 256
ROPE_THETA = 500000.0
ROPE_FRACTION = 4
D_FF = 5504
CONV_WIDTH = 3
Q_BLOCK = 128
EPS = 1e-6
NEG_BIG = -1e30

COL_SIZES = (
    SB_WIDTH, SB_WIDTH, SB_WIDTH,
    DSA_WIDTH, HEAD_DIM, HEAD_DIM,
    IDX_HEADS * IDX_DIM, IDX_DIM,
    IDX_HEADS,
)
IN_COLS = sum(COL_SIZES)
SPLIT_POINTS = tuple(int(v) for v in np.cumsum(COL_SIZES)[:-1])

kernel_name = "hybrid_sb_dsa_convffn_adaln"


def rmsnorm(x, g):
    x32 = x.astype(jnp.float32)
    y = x32 * lax.rsqrt(jnp.mean(x32 * x32, axis=-1, keepdims=True) + EPS)
    return (y * g.astype(jnp.float32)).astype(x.dtype)


def rope_partial(x, pos):
    d = x.shape[-1]
    rd = d // ROPE_FRACTION
    half = rd // 2
    inv_freq = ROPE_THETA ** (-jnp.arange(half, dtype=jnp.float32) / half)
    ang = pos.astype(jnp.float32)[..., None] * inv_freq
    cos = jnp.cos(ang)[:, :, None, :]
    sin = jnp.sin(ang)[:, :, None, :]
    x32 = x.astype(jnp.float32)
    x1, x2, rest = x32[..., :half], x32[..., half:rd], x32[..., rd:]
    out = jnp.concatenate([x1 * cos - x2 * sin, x2 * cos + x1 * sin, rest], axis=-1)
    return out.astype(x.dtype)


def stick_breaking_attention(q, k, v):
    S = q.shape[1]
    scale = q.shape[-1] ** -0.5
    outs = []
    for start in range(0, S, Q_BLOCK):
        end = start + Q_BLOCK
        qb, kb, vb = q[:, start:end], k[:, :end], v[:, :end]
        z = jnp.einsum('bqhd,bkhd->bhqk', qb, kb).astype(jnp.float32) * scale
        t_idx = start + jnp.arange(Q_BLOCK)[:, None]
        s_idx = jnp.arange(end)[None, :]
        strict = s_idx < t_idx
        log_1m = jnp.where(strict, jax.nn.log_sigmoid(-z), 0.0)
        key_axis = log_1m.ndim - 1
        suffix = lax.cumsum(log_1m, axis=key_axis, reverse=True) - log_1m
        w = jnp.where(strict, jnp.exp(jax.nn.log_sigmoid(z) + suffix), 0.0)
        outs.append(jnp.einsum('bhqk,bkhd->bqhd', w.astype(vb.dtype), vb))
    return jnp.concatenate(outs, axis=1)


def dsa_attention(q, k, v, q_idx, k_idx, w_idx):
    S = q.shape[1]
    topk = min(TOPK_MAX, S // 4)
    scale = q.shape[-1] ** -0.5
    gather = jax.vmap(lambda arr, idx: arr[idx])
    outs = []
    for start in range(0, S, Q_BLOCK):
        end = start + Q_BLOCK
        L = max(end, topk)
        rel = jax.nn.relu(jnp.einsum('bqhd,bkd->bqhk', q_idx[:, start:end],
                                     k_idx[:, :L]).astype(jnp.float32))
        score = jnp.einsum('bqh,bqhk->bqk', w_idx[:, start:end].astype(jnp.float32), rel)
        t_idx = start + jnp.arange(Q_BLOCK)[:, None]
        causal = jnp.arange(L)[None, :] <= t_idx
        score = jnp.where(causal, score, NEG_BIG)
        _, sel = lax.top_k(score, topk)
        valid = sel <= t_idx[None]
        k_sel = gather(k, sel)
        v_sel = gather(v, sel)
        logits = jnp.einsum('bqhd,bqkd->bhqk', q[:, start:end], k_sel).astype(jnp.float32) * scale
        logits = jnp.where(valid[:, None], logits, NEG_BIG)
        p = jax.nn.softmax(logits, axis=-1)
        outs.append(jnp.einsum('bhqk,bqkd->bqhd', p.astype(v_sel.dtype), v_sel))
    return jnp.concatenate(outs, axis=1)


def conv_ffn(h, w_up, conv_w, conv_b, w_down):
    S = h.shape[1]
    u = h @ w_up
    up = jnp.pad(u, ((0, 0), (CONV_WIDTH - 1, 0), (0, 0)))
    uc = conv_b
    for i in range(CONV_WIDTH):
        uc = uc + up[:, i:i + S] * conv_w[i]
    gate, val = jnp.split(uc, 2, axis=-1)
    return (jax.nn.silu(gate) * val) @ w_down


def setup_inputs(seed: int = 0) -> dict:
    key = jax.random.key(seed)
    ks = jax.random.split(key, 20)
    D = D_MODEL
    f32 = jnp.float32
    nrm = lambda k, shape, s: jax.random.normal(k, shape, f32) * s
    x = nrm(ks[0], (BATCH, SEQ, D), 1.0)
    c = nrm(ks[1], (BATCH, D), 1.0)
    offset = jax.random.randint(ks[2], (BATCH, 1), 0, 1024, dtype=jnp.int32)
    positions = (offset + jnp.arange(SEQ, dtype=jnp.int32)[None, :]).astype(jnp.int32)
    return {
        "x": x,
        "c": c,
        "positions": positions,
        "w_ada": nrm(ks[3], (DEPTH, D, 6 * D), 0.5 * D ** -0.5),
        "b_ada": nrm(ks[4], (DEPTH, 6 * D), 0.01),
        "norm1_g": 1.0 + nrm(ks[5], (DEPTH, D), 0.01),
        "w_in": nrm(ks[6], (DEPTH, D, IN_COLS), D ** -0.5),
        "sb_norm_g": 1.0 + nrm(ks[7], (DEPTH, SB_WIDTH), 0.01),
        "dsa_norm_g": 1.0 + nrm(ks[8], (DEPTH, DSA_WIDTH), 0.01),
        "w_out": nrm(ks[9], (DEPTH, SB_WIDTH + DSA_WIDTH, D), (SB_WIDTH + DSA_WIDTH) ** -0.5),
        "norm2_g": 1.0 + nrm(ks[10], (DEPTH, D), 0.01),
        "w_up": nrm(ks[11], (DEPTH, D, 2 * D_FF), D ** -0.5),
        "conv_w": nrm(ks[12], (DEPTH, CONV_WIDTH, 2 * D_FF), CONV_WIDTH ** -0.5),
        "conv_b": nrm(ks[13], (DEPTH, 2 * D_FF), 0.01),
        "w_down": nrm(ks[14], (DEPTH, D_FF, D), D_FF ** -0.5),
        "final_norm_g": 1.0 + nrm(ks[15], (D,), 0.01),
    }


def reference(x, c, positions, w_ada, b_ada, norm1_g, w_in, sb_norm_g, dsa_norm_g,
              w_out, norm2_g, w_up, conv_w, conv_b, w_down, final_norm_g):
    B, S, D = x.shape
    idx_scale = (IDX_HEADS ** -0.5) * (IDX_DIM ** -0.5)
    for l in range(DEPTH):
        mod = jax.nn.silu(c) @ w_ada[l] + b_ada[l]
        sh1, sc1, g1, sh2, sc2, g2 = jnp.split(mod[:, None, :], 6, axis=-1)

        h = rmsnorm(x, norm1_g[l]) * (1.0 + sc1) + sh1
        proj = h @ w_in[l]
        q_sb, k_sb, v_sb, q_ds, k_ds, v_ds, q_ix, k_ix, w_ix = jnp.split(proj, SPLIT_POINTS, axis=-1)

        q_sb = q_sb.reshape(B, S, N_HEADS_SB, HEAD_DIM)
        k_sb = k_sb.reshape(B, S, N_HEADS_SB, HEAD_DIM)
        v_sb = v_sb.reshape(B, S, N_HEADS_SB, HEAD_DIM)
        o_sb = stick_breaking_attention(q_sb, k_sb, v_sb).reshape(B, S, SB_WIDTH)

        q_ds = rope_partial(q_ds.reshape(B, S, N_HEADS_DSA, HEAD_DIM), positions)
        k_ds = rope_partial(k_ds[:, :, None, :], positions)[:, :, 0, :]
        q_ix = rope_partial(q_ix.reshape(B, S, IDX_HEADS, IDX_DIM), positions)
        k_ix = rope_partial(k_ix[:, :, None, :], positions)[:, :, 0, :]
        o_ds = dsa_attention(q_ds, k_ds, v_ds, q_ix, k_ix, w_ix * idx_scale).reshape(B, S, DSA_WIDTH)

        merged = jnp.concatenate([rmsnorm(o_sb, sb_norm_g[l]), rmsnorm(o_ds, dsa_norm_g[l])], axis=-1)
        x = x + g1 * (merged @ w_out[l])

        h2 = rmsnorm(x, norm2_g[l]) * (1.0 + sc2) + sh2
        x = x + g2 * conv_ffn(h2, w_up[l], conv_w[l], conv_b[l], w_down[l])
    return rmsnorm(x, final_norm_g)
```

```python
import functools

import jax
import jax.numpy as jnp
import numpy as np
from jax import lax
from jax.experimental import pallas as pl
from jax.experimental.pallas import tpu as pltpu

F32 = jnp.float32
BF16 = jnp.bfloat16
I32 = jnp.int32

HEAD_DIM = 128
IDX_HEADS = 16
IDX_DIM = 64
TOPK_MAX = 256
ROPE_THETA = 500000.0
ROPE_FRACTION = 4
CONV_WIDTH = 3
EPS = 1e-6
NEG_BIG = -1e30
INT_MIN = -(2 ** 31)

LANES = 128
VMEM_LIMIT = 56 * 1024 * 1024

NT_DIMS = (((1,), (1,)), ((), ()))


def _cparams(sem):
    return pltpu.CompilerParams(dimension_semantics=sem, vmem_limit_bytes=VMEM_LIMIT)


def _ada_kernel(c_ref, w_ref, b_ref, o_ref):
    c = c_ref[...]
    s = (c * jax.nn.sigmoid(c)).astype(BF16)
    o_ref[...] = jnp.dot(s, w_ref[...].astype(BF16), preferred_element_type=F32) + b_ref[...]


def _ada(c_pad, w_ada, b_ada, tn=1024):
    m, d = c_pad.shape
    n = w_ada.shape[1]
    return pl.pallas_call(
        _ada_kernel,
        out_shape=jax.ShapeDtypeStruct((m, n), F32),
        grid=(n // tn,),
        in_specs=[
            pl.BlockSpec((m, d), lambda j: (0, 0)),
            pl.BlockSpec((d, tn), lambda j: (0, j)),
            pl.BlockSpec((1, tn), lambda j: (0, j)),
        ],
        out_specs=pl.BlockSpec((m, tn), lambda j: (0, j)),
        compiler_params=_cparams(("arbitrary",)),
        name="ada",
    )(c_pad, w_ada, b_ada)


def _rope(xb, c, sn, sp, shift):
    return (xb * c + pltpu.roll(xb, LANES - shift, axis=1) * sn
            + pltpu.roll(xb, shift, axis=1) * sp)


def _proj_kernel(x_ref, mod_ref, g_ref, pos_ref, invf_ref, w_ref,
                 o_ref, vt_ref, wt_ref, h_scr, tab_scr, *, idx_scale):
    j = pl.program_id(1)
    tn = o_ref.shape[1]
    nsub = tn // LANES

    @pl.when(j == 0)
    def _():
        x = x_ref[...]
        ms = jnp.mean(x * x, axis=-1, keepdims=True)
        y = x * lax.rsqrt(ms + EPS) * g_ref[...]
        h = y * (1.0 + mod_ref[0, 1:2, :]) + mod_ref[0, 0:1, :]
        h_scr[...] = h.astype(BF16)
        pos = pos_ref[...].astype(F32)
        lane = lax.broadcasted_iota(I32, (1, LANES), 1)
        a128 = pos * invf_ref[0:1, :]
        c128, s128 = jnp.cos(a128), jnp.sin(a128)
        tab_scr[0] = jnp.where(lane < 32, c128, 1.0)
        tab_scr[1] = jnp.where(lane < 16, -s128, 0.0)
        tab_scr[2] = jnp.where((lane >= 16) & (lane < 32), s128, 0.0)
        l64 = lane & 63
        a64 = pos * invf_ref[1:2, :]
        c64, s64 = jnp.cos(a64), jnp.sin(a64)
        tab_scr[3] = jnp.where(l64 < 16, c64, 1.0)
        tab_scr[4] = jnp.where(l64 < 8, -s64, 0.0)
        tab_scr[5] = jnp.where((l64 >= 8) & (l64 < 16), s64, 0.0)

    acc = jnp.dot(h_scr[...], w_ref[...], preferred_element_type=F32)

    def sub(s):
        return acc[:, s * LANES:(s + 1) * LANES]

    def rope128(xb):
        return _rope(xb, tab_scr[0], tab_scr[1], tab_scr[2], 16)

    def rope64(xb):
        return _rope(xb, tab_scr[3], tab_scr[4], tab_scr[5], 8)

    def store(s, val):
        o_ref[:, s * LANES:(s + 1) * LANES] = val.astype(BF16)

    @pl.when(j < 6)
    def _():
        o_ref[...] = acc.astype(BF16)

    @pl.when((j == 6) | (j == 7))
    def _():
        for s in range(nsub):
            store(s, rope128(sub(s)))

    @pl.when((j == 8) | (j == 9))
    def _():
        for s in range(nsub):
            store(s, rope64(sub(s)))

    @pl.when(j == 10)
    def _():
        store(0, rope128(sub(0)))
        v = sub(1)
        store(1, v)
        vt_ref[0] = v.T.astype(BF16)
        store(2, rope64(sub(2)))
        w = sub(3)
        store(3, w)
        wt_ref[0] = (w * idx_scale).T[0:IDX_HEADS, :]


def _proj(x2d, mod3, g1, pos_col, invf, w_in_p, batch, seq, idx_scale, tm=512, tn=512):
    rows, d = x2d.shape
    ncols = w_in_p.shape[1]
    tiles_per_batch = seq // tm
    kern = functools.partial(_proj_kernel, idx_scale=idx_scale)
    return pl.pallas_call(
        kern,
        out_shape=(
            jax.ShapeDtypeStruct((rows, ncols), BF16),
            jax.ShapeDtypeStruct((batch, HEAD_DIM, seq), BF16),
            jax.ShapeDtypeStruct((batch, IDX_HEADS, seq), F32),
        ),
        grid=(rows // tm, ncols // tn),
        in_specs=[
            pl.BlockSpec((tm, d), lambda i, j: (i, 0)),
            pl.BlockSpec((1, 6, d), lambda i, j: (i // tiles_per_batch, 0, 0)),
            pl.BlockSpec((1, d), lambda i, j: (0, 0)),
            pl.BlockSpec((tm, 1), lambda i, j: (i, 0)),
            pl.BlockSpec((8, LANES), lambda i, j: (0, 0)),
            pl.BlockSpec((d, tn), lambda i, j: (0, j)),
        ],
        out_specs=(
            pl.BlockSpec((tm, tn), lambda i, j: (i, j)),
            pl.BlockSpec((1, HEAD_DIM, tm),
                         lambda i, j: (i // tiles_per_batch, 0, i % tiles_per_batch)),
            pl.BlockSpec((1, IDX_HEADS, tm),
                         lambda i, j: (i // tiles_per_batch, 0, i % tiles_per_batch)),
        ),
        scratch_shapes=[
            pltpu.VMEM((tm, d), BF16),
            pltpu.VMEM((6, tm, LANES), F32),
        ],
        compiler_params=_cparams(("arbitrary", "arbitrary")),
        name="proj",
    )(x2d, mod3, g1, pos_col, invf, w_in_p)


def _sb_kernel(q_ref, k_ref, v_ref, tri_ref, o_ref, *, scale):
    i = pl.program_id(2)
    tq = q_ref.shape[1]
    q = q_ref[0]
    tri = tri_ref[...]

    def block(j, carry, acc, diag):
        start = pl.multiple_of(j * tq, tq)
        k = k_ref[0, pl.ds(start, tq), :]
        v = v_ref[0, pl.ds(start, tq), :]
        z = lax.dot_general(q, k, NT_DIMS, preferred_element_type=F32) * scale
        sp = jnp.log1p(jnp.exp(-jnp.abs(z)))
        lneg = -(jnp.maximum(z, 0.0) + sp)
        lpos = jnp.minimum(z, 0.0) - sp
        if diag:
            strict = (lax.broadcasted_iota(I32, (tq, tq), 1)
                      < lax.broadcasted_iota(I32, (tq, tq), 0))
            lneg = jnp.where(strict, lneg, 0.0)
        hi = lneg.astype(BF16)
        lo = (lneg - hi.astype(F32)).astype(BF16)
        r = jnp.dot(jnp.concatenate([hi, lo], axis=1), tri, preferred_element_type=F32)
        w = jnp.exp(lpos + r[:, :tq] + carry)
        if diag:
            w = jnp.where(strict, w, 0.0)
        acc = acc + jnp.dot(w.astype(BF16), v, preferred_element_type=F32)
        carry = carry + r[:, tq:]
        return carry, acc

    zeros = jnp.zeros((tq, tq), F32)
    carry, acc = block(i, zeros, zeros, True)

    def body(n, state):
        return block(i - 1 - n, state[0], state[1], False)

    carry, acc = lax.fori_loop(0, i, body, (carry, acc))
    o_ref[0] = acc


def _sb(proj3, tri, batch, seq, n_heads, scale, tq=128):
    kern = functools.partial(_sb_kernel, scale=scale)
    return pl.pallas_call(
        kern,
        out_shape=jax.ShapeDtypeStruct((batch, seq, n_heads * HEAD_DIM), F32),
        grid=(batch, n_heads, seq // tq),
        in_specs=[
            pl.BlockSpec((1, tq, HEAD_DIM), lambda b, h, i: (b, i, h)),
            pl.BlockSpec((1, seq, HEAD_DIM), lambda b, h, i: (b, 0, n_heads + h)),
            pl.BlockSpec((1, seq, HEAD_DIM), lambda b, h, i: (b, 0, 2 * n_heads + h)),
            pl.BlockSpec((2 * tq, 2 * tq), lambda b, h, i: (0, 0)),
        ],
        out_specs=pl.BlockSpec((1, tq, HEAD_DIM), lambda b, h, i: (b, i, h)),
        compiler_params=_cparams(("arbitrary", "arbitrary", "arbitrary")),
        name="sb",
    )(proj3, proj3, proj3, tri)


def _dsa_kernel(qd_ref, qi_ref, kd_ref, vt_ref, ki_ref, wt_ref, o_ref,
                qm_scr, qa_scr, key_scr, acc_scr, m_scr, l_scr, tau_scr, cut_scr,
                *, scale, topk, kc, idx_bits):
    i = pl.program_id(1)
    tq = qd_ref.shape[1]
    n_dsa = qd_ref.shape[2] // HEAD_DIM
    n_full = ((i + 1) * tq) // kc
    n_chunks = ((i + 1) * tq + kc - 1) // kc
    t_row = i * tq + lax.broadcasted_iota(I32, (1, tq), 1)

    lane = lax.broadcasted_iota(I32, (tq, LANES), 1)
    for h in range(IDX_HEADS):
        blk = qi_ref[0, :, (h // 2) * LANES:(h // 2 + 1) * LANES]
        keep = (lane < IDX_DIM) if h % 2 == 0 else (lane >= IDX_DIM)
        qm_scr[h * tq:(h + 1) * tq, :] = jnp.where(keep, blk, jnp.zeros_like(blk))
    for h in range(n_dsa):
        qa_scr[h * tq:(h + 1) * tq, :] = qd_ref[0, :, h * HEAD_DIM:(h + 1) * HEAD_DIM]

    def score_chunk(c, masked):
        start = pl.multiple_of(c * kc, kc)
        kch = ki_ref[0, pl.ds(start, kc), :]
        sc = jnp.zeros((kc, tq), F32)
        for h in range(IDX_HEADS):
            r = lax.dot_general(kch, qm_scr[h * tq:(h + 1) * tq, :], NT_DIMS,
                                preferred_element_type=F32)
            sc = sc + wt_ref[0, h:h + 1, :] * jnp.maximum(r, 0.0)
        bits = pltpu.bitcast(sc, I32)
        key = jnp.where(bits >= 0, bits, bits ^ 0x7FFFFFFF)
        if masked:
            kidx = start + lax.broadcasted_iota(I32, (kc, tq), 0)
            key = jnp.where(kidx <= t_row, key, INT_MIN)
        key_scr[pl.ds(start, kc), :] = key

    def score_body(c, _):
        score_chunk(c, False)
        return 0

    lax.fori_loop(0, n_chunks - 1, score_body, 0)
    score_chunk(n_chunks - 1, True)

    def count(pred):
        def body(c, cnt):
            start = pl.multiple_of(c * kc, kc)
            hit = pred(key_scr[pl.ds(start, kc), :], start)
            return cnt + jnp.sum(hit.astype(I32).reshape(kc // 8, 8, tq), axis=0)
        cnt8 = lax.fori_loop(0, n_chunks, body, jnp.zeros((8, tq), I32))
        return jnp.sum(cnt8, axis=0, keepdims=True)

    def tau_body(b, tau):
        cand = tau + lax.shift_left(jnp.int32(1), 31 - b)
        cnt = count(lambda key, start: key >= cand)
        return jnp.where(cnt >= topk, cand, tau)

    tau = lax.fori_loop(0, 32, tau_body, jnp.full((1, tq), INT_MIN, I32))
    tau = jnp.maximum(tau, INT_MIN + 1)
    tau_scr[...] = tau
    cut_scr[...] = jnp.full((1, tq), 2 ** idx_bits, I32)
    cnt_ge = count(lambda key, start: key >= tau)

    @pl.when(jnp.max(cnt_ge) > topk)
    def _():
        cnt_gt = count(lambda key, start: key > tau)

        def cut_body(b, cut):
            cand = cut + lax.shift_left(jnp.int32(1), idx_bits - 1 - b)

            def pred(key, start):
                kidx = start + lax.broadcasted_iota(I32, (kc, tq), 0)
                return (key == tau) & (kidx < cand)

            return jnp.where(cnt_gt + count(pred) < topk, cand, cut)

        cut_scr[...] = lax.fori_loop(0, idx_bits, cut_body, jnp.zeros((1, tq), I32))

    m_scr[...] = jnp.full(m_scr.shape, NEG_BIG, F32)
    l_scr[...] = jnp.zeros(l_scr.shape, F32)
    acc_scr[...] = jnp.zeros(acc_scr.shape, F32)
    pair = 2 * tq

    def att_body(c, _):
        start = pl.multiple_of(c * kc, kc)
        key = key_scr[pl.ds(start, kc), :]
        kidx = start + lax.broadcasted_iota(I32, (kc, tq), 0)
        tau_v = tau_scr[...]
        sel = (key > tau_v) | ((key == tau_v) & (kidx <= cut_scr[...]))
        kd = kd_ref[0, pl.ds(start, kc), :]
        vt = vt_ref[0, :, pl.ds(start, kc)]
        for hp in range(n_dsa // 2):
            cols = slice(hp * pair, (hp + 1) * pair)
            s = lax.dot_general(kd, qa_scr[cols, :], NT_DIMS,
                                preferred_element_type=F32) * scale
            s = jnp.concatenate([jnp.where(sel, s[:, :tq], NEG_BIG),
                                 jnp.where(sel, s[:, tq:], NEG_BIG)], axis=1)
            m_old = m_scr[:, cols]
            m_new = jnp.maximum(m_old, jnp.max(s, axis=0, keepdims=True))
            alpha = jnp.exp(m_old - m_new)
            p = jnp.exp(s - m_new)
            l_scr[:, cols] = alpha * l_scr[:, cols] + jnp.sum(p, axis=0, keepdims=True)
            acc_scr[:, cols] = alpha * acc_scr[:, cols] + jnp.dot(
                vt, p.astype(BF16), preferred_element_type=F32)
            m_scr[:, cols] = m_new
        return 0

    lax.fori_loop(0, n_chunks, att_body, 0)

    out_t = acc_scr[...] / l_scr[...]
    for h in range(n_dsa):
        o_ref[0, :, h * HEAD_DIM:(h + 1) * HEAD_DIM] = out_t[:, h * tq:(h + 1) * tq].T


def _dsa(proj3, v_t, w_t, batch, seq, n_dsa, scale, tq=128, kc=256):
    width = n_dsa * HEAD_DIM
    topk = min(TOPK_MAX, seq // 4)
    idx_bits = int(np.ceil(np.log2(seq)))
    kern = functools.partial(_dsa_kernel, scale=scale, topk=topk, kc=kc, idx_bits=idx_bits)
    assert width == 1024 and IDX_HEADS * IDX_DIM == 1024
    misc = 5 * 1024 // LANES
    return pl.pallas_call(
        kern,
        out_shape=jax.ShapeDtypeStruct((batch, seq, width), F32),
        grid=(batch, seq // tq),
        in_specs=[
            pl.BlockSpec((1, tq, width), lambda b, i: (b, i, 3)),
            pl.BlockSpec((1, tq, IDX_HEADS * IDX_DIM), lambda b, i: (b, i, 4)),
            pl.BlockSpec((1, seq, HEAD_DIM), lambda b, i: (b, 0, misc)),
            pl.BlockSpec((1, HEAD_DIM, seq), lambda b, i: (b, 0, 0)),
            pl.BlockSpec((1, seq, LANES), lambda b, i: (b, 0, misc + 2)),
            pl.BlockSpec((1, IDX_HEADS, tq), lambda b, i: (b, 0, i)),
        ],
        out_specs=pl.BlockSpec((1, tq, width), lambda b, i: (b, i, 0)),
        scratch_shapes=[
            pltpu.VMEM((IDX_HEADS * tq, LANES), BF16),
            pltpu.VMEM((n_dsa * tq, HEAD_DIM), BF16),
            pltpu.VMEM((seq, tq), I32),
            pltpu.VMEM((HEAD_DIM, n_dsa * tq), F32),
            pltpu.VMEM((1, n_dsa * tq), F32),
            pltpu.VMEM((1, n_dsa * tq), F32),
            pltpu.VMEM((1, tq), I32),
            pltpu.VMEM((1, tq), I32),
        ],
        compiler_params=_cparams(("arbitrary", "arbitrary")),
        name="dsa",
    )(proj3, proj3, proj3, v_t, proj3, w_t)


def _rms(v, g):
    return v * lax.rsqrt(jnp.mean(v * v, axis=-1, keepdims=True) + EPS) * g


def _outp_kernel(osb_ref, ods_ref, x_ref, mod_ref, gsb_ref, gds_ref, g2_ref, w_ref,
                 x1_ref, h2_ref):
    half = osb_ref.shape[1]
    a = _rms(osb_ref[...], gsb_ref[...]).astype(BF16)
    b = _rms(ods_ref[...], gds_ref[...]).astype(BF16)
    y = (jnp.dot(a, w_ref[0:half, :], preferred_element_type=F32)
         + jnp.dot(b, w_ref[half:, :], preferred_element_type=F32))
    x1 = x_ref[...] + mod_ref[0, 2:3, :] * y
    x1_ref[...] = x1
    h2 = _rms(x1, g2_ref[...]) * (1.0 + mod_ref[0, 4:5, :]) + mod_ref[0, 3:4, :]
    h2_ref[...] = h2.astype(BF16)


def _outp(o_sb, o_ds, x2d, mod3, g_sb, g_ds, g2, w_out, seq, tm=512):
    rows, d = x2d.shape
    half = o_sb.shape[1]
    tiles_per_batch = seq // tm
    return pl.pallas_call(
        _outp_kernel,
        out_shape=(jax.ShapeDtypeStruct((rows, d), F32), jax.ShapeDtypeStruct((rows, d), BF16)),
        grid=(rows // tm,),
        in_specs=[
            pl.BlockSpec((tm, half), lambda i: (i, 0)),
            pl.BlockSpec((tm, half), lambda i: (i, 0)),
            pl.BlockSpec((tm, d), lambda i: (i, 0)),
            pl.BlockSpec((1, 6, d), lambda i: (i // tiles_per_batch, 0, 0)),
            pl.BlockSpec((1, half), lambda i: (0, 0)),
            pl.BlockSpec((1, half), lambda i: (0, 0)),
            pl.BlockSpec((1, d), lambda i: (0, 0)),
            pl.BlockSpec((2 * half, d), lambda i: (0, 0)),
        ],
        out_specs=(pl.BlockSpec((tm, d), lambda i: (i, 0)), pl.BlockSpec((tm, d), lambda i: (i, 0))),
        compiler_params=_cparams(("arbitrary",)),
        name="outp",
    )(o_sb, o_ds, x2d, mod3, g_sb, g_ds, g2, w_out)


def _ffn_kernel(h_ref, halo_ref, x1_ref, mod_ref, wg_ref, wv_ref, cwg_ref, cwv_ref,
                cbg_ref, cbv_ref, wd_ref, gf_ref, o_ref, ug_scr, uv_scr, *, tiles_per_batch,
                final_norm):
    i = pl.program_id(0)
    j = pl.program_id(1)
    tm = h_ref.shape[0]
    pad = halo_ref.shape[0]
    h = h_ref[...]
    halo = jnp.where(i % tiles_per_batch == 0, jnp.zeros_like(halo_ref[...]), halo_ref[...])

    def conv(w_ref, cw_ref, cb_ref, scr):
        scr[pad:, :] = jnp.dot(h, w_ref[...], preferred_element_type=F32)
        scr[0:pad, :] = jnp.dot(halo, w_ref[...], preferred_element_type=F32)
        uc = cb_ref[...] + scr[pad:, :] * cw_ref[CONV_WIDTH - 1:CONV_WIDTH, :]
        for tap in range(CONV_WIDTH - 1):
            back = CONV_WIDTH - 1 - tap
            uc = uc + scr[pad - back:pad - back + tm, :] * cw_ref[tap:tap + 1, :]
        return uc

    gate = conv(wg_ref, cwg_ref, cbg_ref, ug_scr)
    val = conv(wv_ref, cwv_ref, cbv_ref, uv_scr)
    a = (gate * jax.nn.sigmoid(gate) * val).astype(BF16)
    part = jnp.dot(a, wd_ref[...], preferred_element_type=F32)

    @pl.when(j == 0)
    def _():
        o_ref[...] = part

    @pl.when(j > 0)
    def _():
        o_ref[...] += part

    @pl.when(j == pl.num_programs(1) - 1)
    def _():
        x2 = x1_ref[...] + mod_ref[0, 5:6, :] * o_ref[...]
        o_ref[...] = _rms(x2, gf_ref[...]) if final_norm else x2


def _ffn(h2, x1, mod3, w_up_g, w_up_v, cw_g, cw_v, cb_g, cb_v, w_dn, g_f, seq, final_norm,
         tm=512, tn=512):
    rows, d = x1.shape
    dff = w_up_g.shape[1]
    tiles_per_batch = seq // tm
    pad = 16
    kern = functools.partial(_ffn_kernel, tiles_per_batch=tiles_per_batch, final_norm=final_norm)
    return pl.pallas_call(
        kern,
        out_shape=jax.ShapeDtypeStruct((rows, d), F32),
        grid=(rows // tm, dff // tn),
        in_specs=[
            pl.BlockSpec((tm, d), lambda i, j: (i, 0)),
            pl.BlockSpec((pad, d), lambda i, j: (jnp.maximum(i * (tm // pad) - 1, 0), 0)),
            pl.BlockSpec((tm, d), lambda i, j: (i, 0)),
            pl.BlockSpec((1, 6, d), lambda i, j: (i // tiles_per_batch, 0, 0)),
            pl.BlockSpec((d, tn), lambda i, j: (0, j)),
            pl.BlockSpec((d, tn), lambda i, j: (0, j)),
            pl.BlockSpec((CONV_WIDTH, tn), lambda i, j: (0, j)),
            pl.BlockSpec((CONV_WIDTH, tn), lambda i, j: (0, j)),
            pl.BlockSpec((1, tn), lambda i, j: (0, j)),
            pl.BlockSpec((1, tn), lambda i, j: (0, j)),
            pl.BlockSpec((tn, d), lambda i, j: (j, 0)),
            pl.BlockSpec((1, d), lambda i, j: (0, 0)),
        ],
        out_specs=pl.BlockSpec((tm, d), lambda i, j: (i, 0)),
        scratch_shapes=[pltpu.VMEM((tm + pad, tn), F32), pltpu.VMEM((tm + pad, tn), F32)],
        compiler_params=_cparams(("arbitrary", "arbitrary")),
        name="ffn",
    )(h2, h2, x1, mod3, w_up_g, w_up_v, cw_g, cw_v, cb_g, cb_v, w_dn, g_f)


def _pad_cols(a, n):
    return jnp.pad(a, ((0, 0), (0, n - a.shape[1])))


def _regroup_w_in(w, d_model):
    sbw = d_model // 2
    p = np.cumsum([0, sbw, sbw, sbw, sbw, HEAD_DIM, HEAD_DIM, IDX_HEADS * IDX_DIM, IDX_DIM, IDX_HEADS])
    q_sb, k_sb, v_sb, q_ds, k_ds, v_ds, q_ix, k_ix, w_ix = [w[:, p[n]:p[n + 1]] for n in range(9)]
    return jnp.concatenate(
        [q_sb, k_sb, v_sb, q_ds, q_ix, k_ds, v_ds, k_ix, k_ix, _pad_cols(w_ix, LANES)], axis=1)


def kernel(x, c, positions, w_ada, b_ada, norm1_g, w_in, sb_norm_g, dsa_norm_g, w_out, norm2_g,
           w_up, conv_w, conv_b, w_down, final_norm_g):
    batch, seq, d = x.shape
    depth = w_ada.shape[0]
    n_sb = (d // HEAD_DIM) // 2
    n_dsa = d // HEAD_DIM - n_sb
    d_ff = w_down.shape[1]
    ff_pad = -(-d_ff // 512) * 512
    idx_scale = (IDX_HEADS ** -0.5) * (IDX_DIM ** -0.5)
    scale = HEAD_DIM ** -0.5
    rows = batch * seq

    half128 = HEAD_DIM // ROPE_FRACTION // 2
    half64 = IDX_DIM // ROPE_FRACTION // 2
    f128 = ROPE_THETA ** (-jnp.arange(half128, dtype=F32) / half128)
    f64 = ROPE_THETA ** (-jnp.arange(half64, dtype=F32) / half64)
    invf = jnp.zeros((8, LANES), F32)
    invf = invf.at[0].set(jnp.tile(f128, LANES // half128)).at[1].set(jnp.tile(f64, LANES // half64))

    tq = 128
    tri_np = np.zeros((2 * tq, 2 * tq), np.float32)
    upper = (np.arange(tq)[:, None] > np.arange(tq)[None, :]).astype(np.float32)
    tri_np[:tq, :tq] = upper
    tri_np[tq:, :tq] = upper
    tri_np[:, tq:] = 1.0
    tri = jnp.asarray(tri_np, BF16)

    pos_col = positions.reshape(rows, 1)
    c_pad = jnp.pad(c, ((0, 16 - batch), (0, 0)))
    x2d = x.reshape(rows, d)

    for l in range(depth):
        mod = _ada(c_pad, w_ada[l], b_ada[l][None, :])[:batch]
        mod3 = mod.reshape(batch, 6, d)

        w_in_p = _regroup_w_in(w_in[l], d).astype(BF16)
        proj, v_t, w_t = _proj(x2d, mod3, norm1_g[l][None, :], pos_col, invf, w_in_p,
                               batch, seq, idx_scale)
        proj3 = proj.reshape(batch, seq, proj.shape[1])

        o_sb = _sb(proj3, tri, batch, seq, n_sb, scale)
        o_ds = _dsa(proj3, v_t, w_t, batch, seq, n_dsa, scale)

        x1, h2 = _outp(o_sb.reshape(rows, -1), o_ds.reshape(rows, -1), x2d, mod3,
                       sb_norm_g[l][None, :], dsa_norm_g[l][None, :], norm2_g[l][None, :],
                       w_out[l].astype(BF16), seq)

        wu = w_up[l]
        w_up_g = _pad_cols(wu[:, :d_ff], ff_pad).astype(BF16)
        w_up_v = _pad_cols(wu[:, d_ff:], ff_pad).astype(BF16)
        cw_g = _pad_cols(conv_w[l][:, :d_ff], ff_pad)
        cw_v = _pad_cols(conv_w[l][:, d_ff:], ff_pad)
        cb_g = _pad_cols(conv_b[l][None, :d_ff], ff_pad)
        cb_v = _pad_cols(conv_b[l][None, d_ff:], ff_pad)
        w_dn = jnp.pad(w_down[l], ((0, ff_pad - d_ff), (0, 0))).astype(BF16)
        x2d = _ffn(h2, x1, mod3, w_up_g, w_up_v, cw_g, cw_v, cb_g, cb_v, w_dn,
                   final_norm_g[None, :], seq, final_norm=(l == depth - 1))
    return x2d.reshape(batch, seq, d)
```

```python
import functools

import jax
import jax.numpy as jnp
import numpy as np
from jax import lax
from jax.experimental import pallas as pl
from jax.experimental.pallas import tpu as pltpu

F32 = jnp.float32
BF16 = jnp.bfloat16
I32 = jnp.int32

HEAD_DIM = 128
IDX_HEADS = 16
IDX_DIM = 64
TOPK_MAX = 256
ROPE_THETA = 500000.0
ROPE_FRACTION = 4
CONV_WIDTH = 3
EPS = 1e-6
NEG_BIG = -1e30
INT_MIN = -(2 ** 31)
LOG2E = 1.4426950408889634

LANES = 128
VMEM_LIMIT = 56 * 1024 * 1024

NT_DIMS = (((1,), (1,)), ((), ()))


def _cparams(sem):
    return pltpu.CompilerParams(dimension_semantics=sem, vmem_limit_bytes=VMEM_LIMIT)


def _ada_kernel(c_ref, w_ref, b_ref, o_ref):
    c = c_ref[...]
    s = (c * jax.nn.sigmoid(c)).astype(BF16)
    o_ref[...] = jnp.dot(s, w_ref[...].astype(BF16), preferred_element_type=F32) + b_ref[...]


def _ada(c_pad, w_ada, b_ada, tn=1024):
    m, d = c_pad.shape
    n = w_ada.shape[1]
    return pl.pallas_call(
        _ada_kernel,
        out_shape=jax.ShapeDtypeStruct((m, n), F32),
        grid=(n // tn,),
        in_specs=[
            pl.BlockSpec((m, d), lambda j: (0, 0)),
            pl.BlockSpec((d, tn), lambda j: (0, j)),
            pl.BlockSpec((1, tn), lambda j: (0, j)),
        ],
        out_specs=pl.BlockSpec((m, tn), lambda j: (0, j)),
        compiler_params=_cparams(("arbitrary",)),
        name="ada",
    )(c_pad, w_ada, b_ada)


def _rope(xb, c, sn, sp, shift):
    return (xb * c + pltpu.roll(xb, LANES - shift, axis=1) * sn
            + pltpu.roll(xb, shift, axis=1) * sp)


def _proj_kernel(x_ref, mod_ref, g_ref, pos_ref, invf_ref, w_ref,
                 o_ref, vt_ref, wt_ref, h_scr, tab_scr, *, idx_scale):
    j = pl.program_id(1)
    tn = o_ref.shape[1]
    nsub = tn // LANES

    @pl.when(j == 0)
    def _():
        x = x_ref[...]
        ms = jnp.mean(x * x, axis=-1, keepdims=True)
        y = x * lax.rsqrt(ms + EPS) * g_ref[...]
        h = y * (1.0 + mod_ref[0, 1:2, :]) + mod_ref[0, 0:1, :]
        h_scr[...] = h.astype(BF16)
        pos = pos_ref[...].astype(F32)
        lane = lax.broadcasted_iota(I32, (1, LANES), 1)
        a128 = pos * invf_ref[0:1, :]
        c128, s128 = jnp.cos(a128), jnp.sin(a128)
        tab_scr[0] = jnp.where(lane < 32, c128, 1.0)
        tab_scr[1] = jnp.where(lane < 16, -s128, 0.0)
        tab_scr[2] = jnp.where((lane >= 16) & (lane < 32), s128, 0.0)
        l64 = lane & 63
        a64 = pos * invf_ref[1:2, :]
        c64, s64 = jnp.cos(a64), jnp.sin(a64)
        tab_scr[3] = jnp.where(l64 < 16, c64, 1.0)
        tab_scr[4] = jnp.where(l64 < 8, -s64, 0.0)
        tab_scr[5] = jnp.where((l64 >= 8) & (l64 < 16), s64, 0.0)

    acc = jnp.dot(h_scr[...], w_ref[...], preferred_element_type=F32)

    def sub(s):
        return acc[:, s * LANES:(s + 1) * LANES]

    def rope128(xb):
        return _rope(xb, tab_scr[0], tab_scr[1], tab_scr[2], 16)

    def rope64(xb):
        return _rope(xb, tab_scr[3], tab_scr[4], tab_scr[5], 8)

    def store(s, val):
        o_ref[:, s * LANES:(s + 1) * LANES] = val.astype(BF16)

    @pl.when(j < 6)
    def _():
        o_ref[...] = acc.astype(BF16)

    @pl.when((j == 6) | (j == 7))
    def _():
        for s in range(nsub):
            store(s, rope128(sub(s)))

    @pl.when((j == 8) | (j == 9))
    def _():
        for s in range(nsub):
            store(s, rope64(sub(s)))

    @pl.when(j == 10)
    def _():
        store(0, rope128(sub(0)))
        v = sub(1)
        store(1, v)
        vt_ref[0] = v.T.astype(BF16)
        store(2, rope64(sub(2)))
        w = sub(3)
        store(3, w)
        wt_ref[0] = (w * idx_scale).T[0:IDX_HEADS, :]


def _proj(x2d, mod3, g1, pos_col, invf, w_in_p, batch, seq, idx_scale, tm=512, tn=512):
    rows, d = x2d.shape
    ncols = w_in_p.shape[1]
    tiles_per_batch = seq // tm
    kern = functools.partial(_proj_kernel, idx_scale=idx_scale)
    return pl.pallas_call(
        kern,
        out_shape=(
            jax.ShapeDtypeStruct((rows, ncols), BF16),
            jax.ShapeDtypeStruct((batch, HEAD_DIM, seq), BF16),
            jax.ShapeDtypeStruct((batch, IDX_HEADS, seq), F32),
        ),
        grid=(rows // tm, ncols // tn),
        in_specs=[
            pl.BlockSpec((tm, d), lambda i, j: (i, 0)),
            pl.BlockSpec((1, 6, d), lambda i, j: (i // tiles_per_batch, 0, 0)),
            pl.BlockSpec((1, d), lambda i, j: (0, 0)),
            pl.BlockSpec((tm, 1), lambda i, j: (i, 0)),
            pl.BlockSpec((8, LANES), lambda i, j: (0, 0)),
            pl.BlockSpec((d, tn), lambda i, j: (0, j)),
        ],
        out_specs=(
            pl.BlockSpec((tm, tn), lambda i, j: (i, j)),
            pl.BlockSpec((1, HEAD_DIM, tm),
                         lambda i, j: (i // tiles_per_batch, 0, i % tiles_per_batch)),
            pl.BlockSpec((1, IDX_HEADS, tm),
                         lambda i, j: (i // tiles_per_batch, 0, i % tiles_per_batch)),
        ),
        scratch_shapes=[
            pltpu.VMEM((tm, d), BF16),
            pltpu.VMEM((6, tm, LANES), F32),
        ],
        compiler_params=_cparams(("arbitrary", "arbitrary")),
        name="proj",
    )(x2d, mod3, g1, pos_col, invf, w_in_p)


SUB = 128


def _sb_kernel(q_ref, k_ref, v_ref, tri_ref, o_ref, carry_scr, acc_scr, *, scale, kc):
    i = pl.program_id(2)
    tq = q_ref.shape[1]
    nsub = kc // SUB
    n_diag = tq // kc
    q = q_ref[0]
    tri = tri_ref[...]
    t_row = i * tq + lax.broadcasted_iota(I32, (tq, kc), 0)

    def chunk(j, diag):
        start = pl.multiple_of(j * kc, kc)
        k = k_ref[0, pl.ds(start, kc), :]
        v = v_ref[0, pl.ds(start, kc), :]
        zn = lax.dot_general(q, k, NT_DIMS, preferred_element_type=F32) * (-scale)
        sp = jnp.log(1.0 + jnp.exp2(jnp.abs(zn) * (-LOG2E)))
        lneg = jnp.minimum(zn, 0.0) - sp
        lpos = lneg - zn
        if diag:
            strict = (start + lax.broadcasted_iota(I32, (tq, kc), 1)) < t_row
            lneg = jnp.where(strict, lneg, 0.0)
        hi = lneg.astype(BF16)
        lo = (lneg - hi.astype(F32)).astype(BF16)
        carry = carry_scr[...]
        parts = [None] * nsub
        for s in reversed(range(nsub)):
            cols = slice(s * SUB, (s + 1) * SUB)
            r = jnp.dot(jnp.concatenate([hi[:, cols], lo[:, cols]], axis=1), tri,
                        preferred_element_type=F32)
            parts[s] = lpos[:, cols] + r[:, :SUB] + carry
            carry = carry + r[:, SUB:]
        carry_scr[...] = carry
        w = jnp.exp(jnp.concatenate(parts, axis=1))
        if diag:
            w = jnp.where(strict, w, 0.0)
        acc_scr[...] += jnp.dot(w.astype(BF16), v, preferred_element_type=F32)

    carry_scr[...] = jnp.zeros(carry_scr.shape, F32)
    acc_scr[...] = jnp.zeros(acc_scr.shape, F32)
    first = i * n_diag
    for n in range(n_diag):
        chunk(first + n_diag - 1 - n, True)

    def body(n, _):
        for u in range(n_diag):
            chunk(first - 1 - n_diag * n - u, False)
        return 0

    lax.fori_loop(0, i, body, 0)
    o_ref[0] = acc_scr[...]


def _sb(proj3, tri, batch, seq, n_heads, scale, tq=512, kc=256):
    kern = functools.partial(_sb_kernel, scale=scale, kc=kc)
    return pl.pallas_call(
        kern,
        out_shape=jax.ShapeDtypeStruct((batch, seq, n_heads * HEAD_DIM), F32),
        grid=(batch, n_heads, seq // tq),
        in_specs=[
            pl.BlockSpec((1, tq, HEAD_DIM), lambda b, h, i: (b, i, h)),
            pl.BlockSpec((1, seq, HEAD_DIM), lambda b, h, i: (b, 0, n_heads + h)),
            pl.BlockSpec((1, seq, HEAD_DIM), lambda b, h, i: (b, 0, 2 * n_heads + h)),
            pl.BlockSpec((2 * SUB, 2 * SUB), lambda b, h, i: (0, 0)),
        ],
        out_specs=pl.BlockSpec((1, tq, HEAD_DIM), lambda b, h, i: (b, i, h)),
        scratch_shapes=[pltpu.VMEM((tq, SUB), F32), pltpu.VMEM((tq, HEAD_DIM), F32)],
        compiler_params=_cparams(("arbitrary", "arbitrary", "arbitrary")),
        name="sb",
    )(proj3, proj3, proj3, tri)


def _dsa_kernel(qd_ref, qi_ref, kd_ref, vt_ref, ki_ref, wt_ref, o_ref,
                qm_scr, qa_scr, key_scr, acc_scr, m_scr, l_scr, tau_scr, cut_scr,
                *, scale, topk, kc, idx_bits):
    i = pl.program_id(1)
    tq = qd_ref.shape[1]
    n_dsa = qd_ref.shape[2] // HEAD_DIM
    n_full = ((i + 1) * tq) // kc
    n_chunks = ((i + 1) * tq + kc - 1) // kc
    t_row = i * tq + lax.broadcasted_iota(I32, (1, tq), 1)

    lane = lax.broadcasted_iota(I32, (tq, LANES), 1)
    for h in range(IDX_HEADS):
        blk = qi_ref[0, :, (h // 2) * LANES:(h // 2 + 1) * LANES]
        keep = (lane < IDX_DIM) if h % 2 == 0 else (lane >= IDX_DIM)
        qm_scr[h * tq:(h + 1) * tq, :] = jnp.where(keep, blk, jnp.zeros_like(blk))
    for h in range(n_dsa):
        qa_scr[h * tq:(h + 1) * tq, :] = qd_ref[0, :, h * HEAD_DIM:(h + 1) * HEAD_DIM]

    def score_chunk(c, masked):
        start = pl.multiple_of(c * kc, kc)
        kch = ki_ref[0, pl.ds(start, kc), :]
        r = lax.dot_general(kch, qm_scr[...], NT_DIMS, preferred_element_type=F32)
        sc = jnp.zeros((kc, tq), F32)
        for h in range(IDX_HEADS):
            sc = sc + wt_ref[0, h:h + 1, :] * jnp.maximum(r[:, h * tq:(h + 1) * tq], 0.0)
        bits = pltpu.bitcast(sc, I32)
        key = jnp.where(bits >= 0, bits, bits ^ 0x7FFFFFFF)
        if masked:
            kidx = start + lax.broadcasted_iota(I32, (kc, tq), 0)
            key = jnp.where(kidx <= t_row, key, INT_MIN)
        key_scr[pl.ds(start, kc), :] = key

    def score_body(c, _):
        score_chunk(c, False)
        return 0

    lax.fori_loop(0, n_chunks - 1, score_body, 0)
    score_chunk(n_chunks - 1, True)

    def count(pred):
        def body(c, cnt):
            start = pl.multiple_of(c * kc, kc)
            hit = pred(key_scr[pl.ds(start, kc), :], start)
            return cnt + jnp.sum(hit.astype(I32).reshape(kc // 8, 8, tq), axis=0)
        cnt8 = lax.fori_loop(0, n_chunks, body, jnp.zeros((8, tq), I32))
        return jnp.sum(cnt8, axis=0, keepdims=True)

    def tau_body(b, tau):
        cand = tau + lax.shift_left(jnp.int32(1), 31 - b)
        cnt = count(lambda key, start: key >= cand)
        return jnp.where(cnt >= topk, cand, tau)

    tau = lax.fori_loop(0, 32, tau_body, jnp.full((1, tq), INT_MIN, I32))
    tau = jnp.maximum(tau, INT_MIN + 1)
    tau_scr[...] = tau
    cut_scr[...] = jnp.full((1, tq), 2 ** idx_bits, I32)
    cnt_ge = count(lambda key, start: key >= tau)

    @pl.when(jnp.max(cnt_ge) > topk)
    def _():
        cnt_gt = count(lambda key, start: key > tau)

        def cut_body(b, cut):
            cand = cut + lax.shift_left(jnp.int32(1), idx_bits - 1 - b)

            def pred(key, start):
                kidx = start + lax.broadcasted_iota(I32, (kc, tq), 0)
                return (key == tau) & (kidx < cand)

            return jnp.where(cnt_gt + count(pred) < topk, cand, cut)

        cut_scr[...] = lax.fori_loop(0, idx_bits, cut_body, jnp.zeros((1, tq), I32))

    m_scr[...] = jnp.full(m_scr.shape, NEG_BIG, F32)
    l_scr[...] = jnp.zeros(l_scr.shape, F32)
    acc_scr[...] = jnp.zeros(acc_scr.shape, F32)
    c2 = scale * LOG2E

    def att_body(c, _):
        start = pl.multiple_of(c * kc, kc)
        key = key_scr[pl.ds(start, kc), :]
        kidx = start + lax.broadcasted_iota(I32, (kc, tq), 0)
        tau_v = tau_scr[...]
        sel = (key > tau_v) | ((key == tau_v) & (kidx <= cut_scr[...]))
        kd = kd_ref[0, pl.ds(start, kc), :]
        vt = vt_ref[0, :, pl.ds(start, kc)]
        s = lax.dot_general(kd, qa_scr[...], NT_DIMS, preferred_element_type=F32)
        s = jnp.concatenate([jnp.where(sel, s[:, h * tq:(h + 1) * tq], NEG_BIG)
                             for h in range(n_dsa)], axis=1)
        m_old = m_scr[...]
        m_new = jnp.maximum(m_old, jnp.max(s, axis=0, keepdims=True))
        alpha = jnp.exp2((m_old - m_new) * c2)
        p = jnp.exp2((s - m_new) * c2)
        l_scr[...] = alpha * l_scr[...] + jnp.sum(p, axis=0, keepdims=True)
        acc_scr[...] = alpha * acc_scr[...] + jnp.dot(vt, p.astype(BF16),
                                                      preferred_element_type=F32)
        m_scr[...] = m_new
        return 0

    lax.fori_loop(0, n_chunks, att_body, 0)

    out_t = acc_scr[...] / l_scr[...]
    for h in range(n_dsa):
        o_ref[0, :, h * HEAD_DIM:(h + 1) * HEAD_DIM] = out_t[:, h * tq:(h + 1) * tq].T


def _dsa(proj3, v_t, w_t, batch, seq, n_dsa, scale, tq=128, kc=256):
    width = n_dsa * HEAD_DIM
    topk = min(TOPK_MAX, seq // 4)
    idx_bits = int(np.ceil(np.log2(seq)))
    kern = functools.partial(_dsa_kernel, scale=scale, topk=topk, kc=kc, idx_bits=idx_bits)
    assert width == 1024 and IDX_HEADS * IDX_DIM == 1024
    misc = 5 * 1024 // LANES
    return pl.pallas_call(
        kern,
        out_shape=jax.ShapeDtypeStruct((batch, seq, width), F32),
        grid=(batch, seq // tq),
        in_specs=[
            pl.BlockSpec((1, tq, width), lambda b, i: (b, i, 3)),
            pl.BlockSpec((1, tq, IDX_HEADS * IDX_DIM), lambda b, i: (b, i, 4)),
            pl.BlockSpec((1, seq, HEAD_DIM), lambda b, i: (b, 0, misc)),
            pl.BlockSpec((1, HEAD_DIM, seq), lambda b, i: (b, 0, 0)),
            pl.BlockSpec((1, seq, LANES), lambda b, i: (b, 0, misc + 2)),
            pl.BlockSpec((1, IDX_HEADS, tq), lambda b, i: (b, 0, i)),
        ],
        out_specs=pl.BlockSpec((1, tq, width), lambda b, i: (b, i, 0)),
        scratch_shapes=[
            pltpu.VMEM((IDX_HEADS * tq, LANES), BF16),
            pltpu.VMEM((n_dsa * tq, HEAD_DIM), BF16),
            pltpu.VMEM((seq, tq), I32),
            pltpu.VMEM((HEAD_DIM, n_dsa * tq), F32),
            pltpu.VMEM((1, n_dsa * tq), F32),
            pltpu.VMEM((1, n_dsa * tq), F32),
            pltpu.VMEM((1, tq), I32),
            pltpu.VMEM((1, tq), I32),
        ],
        compiler_params=_cparams(("arbitrary", "arbitrary")),
        name="dsa",
    )(proj3, proj3, proj3, v_t, proj3, w_t)


def _rms(v, g):
    return v * lax.rsqrt(jnp.mean(v * v, axis=-1, keepdims=True) + EPS) * g


def _outp_kernel(osb_ref, ods_ref, x_ref, mod_ref, gsb_ref, gds_ref, g2_ref, w_ref,
                 x1_ref, h2_ref):
    half = osb_ref.shape[1]
    a = _rms(osb_ref[...], gsb_ref[...]).astype(BF16)
    b = _rms(ods_ref[...], gds_ref[...]).astype(BF16)
    y = (jnp.dot(a, w_ref[0:half, :], preferred_element_type=F32)
         + jnp.dot(b, w_ref[half:, :], preferred_element_type=F32))
    x1 = x_ref[...] + mod_ref[0, 2:3, :] * y
    x1_ref[...] = x1
    h2 = _rms(x1, g2_ref[...]) * (1.0 + mod_ref[0, 4:5, :]) + mod_ref[0, 3:4, :]
    h2_ref[...] = h2.astype(BF16)


def _outp(o_sb, o_ds, x2d, mod3, g_sb, g_ds, g2, w_out, seq, tm=512):
    rows, d = x2d.shape
    half = o_sb.shape[1]
    tiles_per_batch = seq // tm
    return pl.pallas_call(
        _outp_kernel,
        out_shape=(jax.ShapeDtypeStruct((rows, d), F32), jax.ShapeDtypeStruct((rows, d), BF16)),
        grid=(rows // tm,),
        in_specs=[
            pl.BlockSpec((tm, half), lambda i: (i, 0)),
            pl.BlockSpec((tm, half), lambda i: (i, 0)),
            pl.BlockSpec((tm, d), lambda i: (i, 0)),
            pl.BlockSpec((1, 6, d), lambda i: (i // tiles_per_batch, 0, 0)),
            pl.BlockSpec((1, half), lambda i: (0, 0)),
            pl.BlockSpec((1, half), lambda i: (0, 0)),
            pl.BlockSpec((1, d), lambda i: (0, 0)),
            pl.BlockSpec((2 * half, d), lambda i: (0, 0)),
        ],
        out_specs=(pl.BlockSpec((tm, d), lambda i: (i, 0)), pl.BlockSpec((tm, d), lambda i: (i, 0))),
        compiler_params=_cparams(("arbitrary",)),
        name="outp",
    )(o_sb, o_ds, x2d, mod3, g_sb, g_ds, g2, w_out)


def _ffn_kernel(h_ref, halo_ref, x1_ref, mod_ref, wg_ref, wv_ref, cwg_ref, cwv_ref,
                cbg_ref, cbv_ref, wd_ref, gf_ref, o_ref, ug_scr, uv_scr, *, tiles_per_batch,
                final_norm):
    i = pl.program_id(0)
    j = pl.program_id(1)
    tm = h_ref.shape[0]
    pad = halo_ref.shape[0]
    h = h_ref[...]
    halo = jnp.where(i % tiles_per_batch == 0, jnp.zeros_like(halo_ref[...]), halo_ref[...])

    def conv(w_ref, cw_ref, cb_ref, scr):
        scr[pad:, :] = jnp.dot(h, w_ref[...], preferred_element_type=F32)
        scr[0:pad, :] = jnp.dot(halo, w_ref[...], preferred_element_type=F32)
        uc = cb_ref[...] + scr[pad:, :] * cw_ref[CONV_WIDTH - 1:CONV_WIDTH, :]
        for tap in range(CONV_WIDTH - 1):
            back = CONV_WIDTH - 1 - tap
            uc = uc + scr[pad - back:pad - back + tm, :] * cw_ref[tap:tap + 1, :]
        return uc

    gate = conv(wg_ref, cwg_ref, cbg_ref, ug_scr)
    val = conv(wv_ref, cwv_ref, cbv_ref, uv_scr)
    a = (gate * jax.nn.sigmoid(gate) * val).astype(BF16)
    part = jnp.dot(a, wd_ref[...], preferred_element_type=F32)

    @pl.when(j == 0)
    def _():
        o_ref[...] = part

    @pl.when(j > 0)
    def _():
        o_ref[...] += part

    @pl.when(j == pl.num_programs(1) - 1)
    def _():
        x2 = x1_ref[...] + mod_ref[0, 5:6, :] * o_ref[...]
        o_ref[...] = _rms(x2, gf_ref[...]) if final_norm else x2


def _ffn(h2, x1, mod3, w_up_g, w_up_v, cw_g, cw_v, cb_g, cb_v, w_dn, g_f, seq, final_norm,
         tm=512, tn=512):
    rows, d = x1.shape
    dff = w_up_g.shape[1]
    tiles_per_batch = seq // tm
    pad = 16
    kern = functools.partial(_ffn_kernel, tiles_per_batch=tiles_per_batch, final_norm=final_norm)
    return pl.pallas_call(
        kern,
        out_shape=jax.ShapeDtypeStruct((rows, d), F32),
        grid=(rows // tm, dff // tn),
        in_specs=[
            pl.BlockSpec((tm, d), lambda i, j: (i, 0)),
            pl.BlockSpec((pad, d), lambda i, j: (jnp.maximum(i * (tm // pad) - 1, 0), 0)),
            pl.BlockSpec((tm, d), lambda i, j: (i, 0)),
            pl.BlockSpec((1, 6, d), lambda i, j: (i // tiles_per_batch, 0, 0)),
            pl.BlockSpec((d, tn), lambda i, j: (0, j)),
            pl.BlockSpec((d, tn), lambda i, j: (0, j)),
            pl.BlockSpec((CONV_WIDTH, tn), lambda i, j: (0, j)),
            pl.BlockSpec((CONV_WIDTH, tn), lambda i, j: (0, j)),
            pl.BlockSpec((1, tn), lambda i, j: (0, j)),
            pl.BlockSpec((1, tn), lambda i, j: (0, j)),
            pl.BlockSpec((tn, d), lambda i, j: (j, 0)),
            pl.BlockSpec((1, d), lambda i, j: (0, 0)),
        ],
        out_specs=pl.BlockSpec((tm, d), lambda i, j: (i, 0)),
        scratch_shapes=[pltpu.VMEM((tm + pad, tn), F32), pltpu.VMEM((tm + pad, tn), F32)],
        compiler_params=_cparams(("arbitrary", "arbitrary")),
        name="ffn",
    )(h2, h2, x1, mod3, w_up_g, w_up_v, cw_g, cw_v, cb_g, cb_v, w_dn, g_f)


def _pad_cols(a, n):
    return jnp.pad(a, ((0, 0), (0, n - a.shape[1])))


def _regroup_w_in(w, d_model):
    sbw = d_model // 2
    p = np.cumsum([0, sbw, sbw, sbw, sbw, HEAD_DIM, HEAD_DIM, IDX_HEADS * IDX_DIM, IDX_DIM, IDX_HEADS])
    q_sb, k_sb, v_sb, q_ds, k_ds, v_ds, q_ix, k_ix, w_ix = [w[:, p[n]:p[n + 1]] for n in range(9)]
    return jnp.concatenate(
        [q_sb, k_sb, v_sb, q_ds, q_ix, k_ds, v_ds, k_ix, k_ix, _pad_cols(w_ix, LANES)], axis=1)


def kernel(x, c, positions, w_ada, b_ada, norm1_g, w_in, sb_norm_g, dsa_norm_g, w_out, norm2_g,
           w_up, conv_w, conv_b, w_down, final_norm_g):
    batch, seq, d = x.shape
    depth = w_ada.shape[0]
    n_sb = (d // HEAD_DIM) // 2
    n_dsa = d // HEAD_DIM - n_sb
    d_ff = w_down.shape[1]
    ff_pad = -(-d_ff // 512) * 512
    idx_scale = (IDX_HEADS ** -0.5) * (IDX_DIM ** -0.5)
    scale = HEAD_DIM ** -0.5
    rows = batch * seq

    half128 = HEAD_DIM // ROPE_FRACTION // 2
    half64 = IDX_DIM // ROPE_FRACTION // 2
    f128 = ROPE_THETA ** (-jnp.arange(half128, dtype=F32) / half128)
    f64 = ROPE_THETA ** (-jnp.arange(half64, dtype=F32) / half64)
    invf = jnp.zeros((8, LANES), F32)
    invf = invf.at[0].set(jnp.tile(f128, LANES // half128)).at[1].set(jnp.tile(f64, LANES // half64))

    tri_np = np.zeros((2 * SUB, 2 * SUB), np.float32)
    later = (np.arange(SUB)[:, None] > np.arange(SUB)[None, :]).astype(np.float32)
    tri_np[:SUB, :SUB] = later
    tri_np[SUB:, :SUB] = later
    tri_np[:, SUB:] = 1.0
    tri = jnp.asarray(tri_np, BF16)

    pos_col = positions.reshape(rows, 1)
    c_pad = jnp.pad(c, ((0, 16 - batch), (0, 0)))
    x2d = x.reshape(rows, d)

    for l in range(depth):
        mod = _ada(c_pad, w_ada[l], b_ada[l][None, :])[:batch]
        mod3 = mod.reshape(batch, 6, d)

        w_in_p = _regroup_w_in(w_in[l], d).astype(BF16)
        proj, v_t, w_t = _proj(x2d, mod3, norm1_g[l][None, :], pos_col, invf, w_in_p,
                               batch, seq, idx_scale)
        proj3 = proj.reshape(batch, seq, proj.shape[1])

        o_sb = _sb(proj3, tri, batch, seq, n_sb, scale)
        o_ds = _dsa(proj3, v_t, w_t, batch, seq, n_dsa, scale)

        x1, h2 = _outp(o_sb.reshape(rows, -1), o_ds.reshape(rows, -1), x2d, mod3,
                       sb_norm_g[l][None, :], dsa_norm_g[l][None, :], norm2_g[l][None, :],
                       w_out[l].astype(BF16), seq)

        wu = w_up[l]
        w_up_g = _pad_cols(wu[:, :d_ff], ff_pad).astype(BF16)
        w_up_v = _pad_cols(wu[:, d_ff:], ff_pad).astype(BF16)
        cw_g = _pad_cols(conv_w[l][:, :d_ff], ff_pad)
        cw_v = _pad_cols(conv_w[l][:, d_ff:], ff_pad)
        cb_g = _pad_cols(conv_b[l][None, :d_ff], ff_pad)
        cb_v = _pad_cols(conv_b[l][None, d_ff:], ff_pad)
        w_dn = jnp.pad(w_down[l], ((0, ff_pad - d_ff), (0, 0))).astype(BF16)
        x2d = _ffn(h2, x1, mod3, w_up_g, w_up_v, cw_g, cw_v, cb_g, cb_v, w_dn,
                   final_norm_g[None, :], seq, final_norm=(l == depth - 1))
    return x2d.reshape(batch, seq, d)
```

```python
import functools

import jax
import jax.numpy as jnp
import numpy as np
from jax import lax
from jax.experimental import pallas as pl
from jax.experimental.pallas import tpu as pltpu

F32 = jnp.float32
BF16 = jnp.bfloat16
I32 = jnp.int32
I16 = jnp.int16

HEAD_DIM = 128
IDX_HEADS = 16
IDX_DIM = 64
TOPK_MAX = 256
ROPE_THETA = 500000.0
ROPE_FRACTION = 4
CONV_WIDTH = 3
EPS = 1e-6
NEG_BIG = -1e30
INT_MIN = -(2 ** 31)
I16_MIN = -(2 ** 15)
LOG2E = 1.4426950408889634

LANES = 128
VMEM_LIMIT = 56 * 1024 * 1024

NT_DIMS = (((1,), (1,)), ((), ()))


def _cparams(sem):
    return pltpu.CompilerParams(dimension_semantics=sem, vmem_limit_bytes=VMEM_LIMIT)


def _ada_kernel(c_ref, w_ref, b_ref, o_ref):
    c = c_ref[...]
    s = (c * jax.nn.sigmoid(c)).astype(BF16)
    o_ref[...] = jnp.dot(s, w_ref[...].astype(BF16), preferred_element_type=F32) + b_ref[...]


def _ada(c_pad, w_ada, b_ada, tn=1024):
    m, d = c_pad.shape
    n = w_ada.shape[1]
    return pl.pallas_call(
        _ada_kernel,
        out_shape=jax.ShapeDtypeStruct((m, n), F32),
        grid=(n // tn,),
        in_specs=[
            pl.BlockSpec((m, d), lambda j: (0, 0)),
            pl.BlockSpec((d, tn), lambda j: (0, j)),
            pl.BlockSpec((1, tn), lambda j: (0, j)),
        ],
        out_specs=pl.BlockSpec((m, tn), lambda j: (0, j)),
        compiler_params=_cparams(("arbitrary",)),
        name="ada",
    )(c_pad, w_ada, b_ada)


def _rope(xb, c, sn, sp, shift):
    return (xb * c + pltpu.roll(xb, LANES - shift, axis=1) * sn
            + pltpu.roll(xb, shift, axis=1) * sp)


def _proj_kernel(x_ref, mod_ref, g_ref, pos_ref, invf_ref, w_ref,
                 o_ref, vt_ref, wt_ref, h_scr, tab_scr, *, idx_scale):
    j = pl.program_id(1)
    tn = o_ref.shape[1]
    nsub = tn // LANES

    @pl.when(j == 0)
    def _():
        x = x_ref[...]
        ms = jnp.mean(x * x, axis=-1, keepdims=True)
        y = x * lax.rsqrt(ms + EPS) * g_ref[...]
        h = y * (1.0 + mod_ref[0, 1:2, :]) + mod_ref[0, 0:1, :]
        h_scr[...] = h.astype(BF16)
        pos = pos_ref[...].astype(F32)
        lane = lax.broadcasted_iota(I32, (1, LANES), 1)
        a128 = pos * invf_ref[0:1, :]
        c128, s128 = jnp.cos(a128), jnp.sin(a128)
        tab_scr[0] = jnp.where(lane < 32, c128, 1.0)
        tab_scr[1] = jnp.where(lane < 16, -s128, 0.0)
        tab_scr[2] = jnp.where((lane >= 16) & (lane < 32), s128, 0.0)
        l64 = lane & 63
        a64 = pos * invf_ref[1:2, :]
        c64, s64 = jnp.cos(a64), jnp.sin(a64)
        tab_scr[3] = jnp.where(l64 < 16, c64, 1.0)
        tab_scr[4] = jnp.where(l64 < 8, -s64, 0.0)
        tab_scr[5] = jnp.where((l64 >= 8) & (l64 < 16), s64, 0.0)

    acc = jnp.dot(h_scr[...], w_ref[...], preferred_element_type=F32)

    def sub(s):
        return acc[:, s * LANES:(s + 1) * LANES]

    def rope128(xb):
        return _rope(xb, tab_scr[0], tab_scr[1], tab_scr[2], 16)

    def rope64(xb):
        return _rope(xb, tab_scr[3], tab_scr[4], tab_scr[5], 8)

    def store(s, val):
        o_ref[:, s * LANES:(s + 1) * LANES] = val.astype(BF16)

    @pl.when(j < 6)
    def _():
        o_ref[...] = acc.astype(BF16)

    @pl.when((j == 6) | (j == 7))
    def _():
        for s in range(nsub):
            store(s, rope128(sub(s)))

    @pl.when((j == 8) | (j == 9))
    def _():
        for s in range(nsub):
            store(s, rope64(sub(s)))

    @pl.when(j == 10)
    def _():
        store(0, rope128(sub(0)))
        v = sub(1)
        store(1, v)
        vt_ref[0] = v.T.astype(BF16)
        store(2, rope64(sub(2)))
        w = sub(3)
        store(3, w)
        wt_ref[0] = (w * idx_scale).T[0:IDX_HEADS, :]


def _proj(x2d, mod3, g1, pos_col, invf, w_in_p, batch, seq, idx_scale, tm=1024, tn=512):
    rows, d = x2d.shape
    ncols = w_in_p.shape[1]
    tiles_per_batch = seq // tm
    kern = functools.partial(_proj_kernel, idx_scale=idx_scale)
    return pl.pallas_call(
        kern,
        out_shape=(
            jax.ShapeDtypeStruct((rows, ncols), BF16),
            jax.ShapeDtypeStruct((batch, HEAD_DIM, seq), BF16),
            jax.ShapeDtypeStruct((batch, IDX_HEADS, seq), F32),
        ),
        grid=(rows // tm, ncols // tn),
        in_specs=[
            pl.BlockSpec((tm, d), lambda i, j: (i, 0)),
            pl.BlockSpec((1, 6, d), lambda i, j: (i // tiles_per_batch, 0, 0)),
            pl.BlockSpec((1, d), lambda i, j: (0, 0)),
            pl.BlockSpec((tm, 1), lambda i, j: (i, 0)),
            pl.BlockSpec((8, LANES), lambda i, j: (0, 0)),
            pl.BlockSpec((d, tn), lambda i, j: (0, j)),
        ],
        out_specs=(
            pl.BlockSpec((tm, tn), lambda i, j: (i, j)),
            pl.BlockSpec((1, HEAD_DIM, tm),
                         lambda i, j: (i // tiles_per_batch, 0, i % tiles_per_batch)),
            pl.BlockSpec((1, IDX_HEADS, tm),
                         lambda i, j: (i // tiles_per_batch, 0, i % tiles_per_batch)),
        ),
        scratch_shapes=[
            pltpu.VMEM((tm, d), BF16),
            pltpu.VMEM((6, tm, LANES), F32),
        ],
        compiler_params=_cparams(("arbitrary", "arbitrary")),
        name="proj",
    )(x2d, mod3, g1, pos_col, invf, w_in_p)


SUB = 128


def _sb_kernel(q_ref, k_ref, v_ref, tri_ref, o_ref, carry_scr, acc_scr, *, scale, kc):
    i = pl.program_id(2)
    tq = q_ref.shape[1]
    nsub = kc // SUB
    n_diag = tq // kc
    q = q_ref[0]
    tri = tri_ref[...]
    t_row = i * tq + lax.broadcasted_iota(I32, (tq, kc), 0)

    def chunk(j, diag):
        start = pl.multiple_of(j * kc, kc)
        k = k_ref[0, pl.ds(start, kc), :]
        v = v_ref[0, pl.ds(start, kc), :]
        zn = lax.dot_general(q, k, NT_DIMS, preferred_element_type=F32) * (-scale)
        sp = jnp.log(1.0 + jnp.exp2(jnp.abs(zn) * (-LOG2E)))
        lneg = jnp.minimum(zn, 0.0) - sp
        lpos = lneg - zn
        if diag:
            strict = (start + lax.broadcasted_iota(I32, (tq, kc), 1)) < t_row
            lneg = jnp.where(strict, lneg, 0.0)
        hi = lneg.astype(BF16)
        lo = (lneg - hi.astype(F32)).astype(BF16)
        carry = carry_scr[...]
        parts = [None] * nsub
        for s in reversed(range(nsub)):
            cols = slice(s * SUB, (s + 1) * SUB)
            r = jnp.dot(jnp.concatenate([hi[:, cols], lo[:, cols]], axis=1), tri,
                        preferred_element_type=F32)
            parts[s] = lpos[:, cols] + r[:, :SUB] + carry
            carry = carry + r[:, SUB:]
        carry_scr[...] = carry
        w = jnp.exp(jnp.concatenate(parts, axis=1))
        if diag:
            w = jnp.where(strict, w, 0.0)
        acc_scr[...] += jnp.dot(w.astype(BF16), v, preferred_element_type=F32)

    carry_scr[...] = jnp.zeros(carry_scr.shape, F32)
    acc_scr[...] = jnp.zeros(acc_scr.shape, F32)
    first = i * n_diag
    for n in range(n_diag):
        chunk(first + n_diag - 1 - n, True)

    def body(n, _):
        for u in range(n_diag):
            chunk(first - 1 - n_diag * n - u, False)
        return 0

    lax.fori_loop(0, i, body, 0)
    o_ref[0] = acc_scr[...]


def _sb(proj3, tri, batch, seq, n_heads, scale, tq=512, kc=256):
    kern = functools.partial(_sb_kernel, scale=scale, kc=kc)
    return pl.pallas_call(
        kern,
        out_shape=jax.ShapeDtypeStruct((batch, seq, n_heads * HEAD_DIM), F32),
        grid=(batch, n_heads, seq // tq),
        in_specs=[
            pl.BlockSpec((1, tq, HEAD_DIM), lambda b, h, i: (b, i, h)),
            pl.BlockSpec((1, seq, HEAD_DIM), lambda b, h, i: (b, 0, n_heads + h)),
            pl.BlockSpec((1, seq, HEAD_DIM), lambda b, h, i: (b, 0, 2 * n_heads + h)),
            pl.BlockSpec((2 * SUB, 2 * SUB), lambda b, h, i: (0, 0)),
        ],
        out_specs=pl.BlockSpec((1, tq, HEAD_DIM), lambda b, h, i: (b, i, h)),
        scratch_shapes=[pltpu.VMEM((tq, SUB), F32), pltpu.VMEM((tq, HEAD_DIM), F32)],
        compiler_params=_cparams(("arbitrary", "arbitrary", "arbitrary")),
        name="sb",
    )(proj3, proj3, proj3, tri)


def _dsa_kernel(qd_ref, qi_ref, kd_ref, vt_ref, ki_ref, wt_ref, o_ref,
                qm_scr, qa_scr, key_scr, hi_scr, lo_scr, acc_scr, m_scr, l_scr, tau_scr,
                r_scr, s_scr, p_scr, alpha_scr, *, scale, topk, kc, idx_bits):
    i = pl.program_id(1)
    tq = qd_ref.shape[1]
    n_dsa = qd_ref.shape[2] // HEAD_DIM
    n_chunks = ((i + 1) * tq + kc - 1) // kc
    ks = 2 * kc
    n_trips = (n_chunks + 1) // 2
    t_row = i * tq + lax.broadcasted_iota(I32, (1, tq), 1)

    lane = lax.broadcasted_iota(I32, (tq, LANES), 1)
    for h in range(IDX_HEADS):
        blk = qi_ref[0, :, (h // 2) * LANES:(h // 2 + 1) * LANES]
        keep = (lane < IDX_DIM) if h % 2 == 0 else (lane >= IDX_DIM)
        qm_scr[h * tq:(h + 1) * tq, :] = jnp.where(keep, blk, jnp.zeros_like(blk))
    for h in range(n_dsa):
        qa_scr[h * tq:(h + 1) * tq, :] = qd_ref[0, :, h * HEAD_DIM:(h + 1) * HEAD_DIM]

    def score_dot(c):
        kch = ki_ref[0, pl.ds(pl.multiple_of(c * kc, kc), kc), :]
        return lax.dot_general(kch, qm_scr[...], NT_DIMS, preferred_element_type=F32)

    def score_chunk(c, masked):
        start = pl.multiple_of(c * kc, kc)
        r = r_scr[...]
        sc = jnp.zeros((kc, tq), F32)
        for h in range(IDX_HEADS):
            sc = sc + wt_ref[0, h:h + 1, :] * jnp.maximum(r[:, h * tq:(h + 1) * tq], 0.0)
        bits = pltpu.bitcast(sc, I32)
        key = jnp.where(bits >= 0, bits, bits ^ 0x7FFFFFFF)
        if masked:
            kidx = start + lax.broadcasted_iota(I32, (kc, tq), 0)
            key = jnp.where(kidx <= t_row, key, INT_MIN)
        key_scr[pl.ds(start, kc), :] = key
        hi_scr[pl.ds(start, kc), :] = (key >> 16).astype(I16)
        lo_scr[pl.ds(start, kc), :] = ((key & 0xFFFF) + I16_MIN).astype(I16)

    r_scr[...] = score_dot(0)

    def score_body(c, _):
        nxt = score_dot(c + 1)
        score_chunk(c, False)
        r_scr[...] = nxt
        return 0

    lax.fori_loop(0, n_chunks - 1, score_body, 0)
    score_chunk(n_chunks - 1, True)
    pad_rows = pl.ds(pl.multiple_of(n_chunks * kc, kc), kc)
    hi_scr[pad_rows, :] = jnp.full((kc, tq), I16_MIN, I16)
    lo_scr[pad_rows, :] = jnp.full((kc, tq), I16_MIN, I16)

    def count16(ref, cand):
        cand16 = cand.astype(I16)

        def body(t, cnt):
            blk = ref[pl.ds(pl.multiple_of(t * ks, ks), ks), :]
            one = jnp.where(blk >= cand16, jnp.int16(1), jnp.int16(0))
            parts = [one[r * 16:(r + 1) * 16, :] for r in range(ks // 16)]
            while len(parts) > 1:
                parts = [a + b for a, b in zip(parts[0::2], parts[1::2])]
            return cnt + parts[0]

        cnt = lax.fori_loop(0, n_trips, body, jnp.zeros((16, tq), I16))
        return jnp.sum(cnt.astype(I32), axis=0, keepdims=True)

    def search16(ref, need):
        def body(b, thr):
            cand = thr + lax.shift_left(jnp.int32(1), 15 - b)
            return jnp.where(count16(ref, cand) >= need, cand, thr)

        return lax.fori_loop(0, 16, body, jnp.full((1, tq), I16_MIN, I32))

    t_hi = search16(hi_scr, topk)
    above = jnp.where(t_hi < -I16_MIN - 1, count16(hi_scr, jnp.minimum(t_hi + 1, -I16_MIN - 1)), 0)
    t_hi16 = t_hi.astype(I16)

    def bucket_body(t, _):
        rows = pl.ds(pl.multiple_of(t * ks, ks), ks)
        lo_scr[rows, :] = jnp.where(hi_scr[rows, :] == t_hi16, lo_scr[rows, :], jnp.int16(I16_MIN))
        return 0

    lax.fori_loop(0, n_trips, bucket_body, 0)
    t_lo = search16(lo_scr, topk - above)
    cnt_ge = above + count16(lo_scr, t_lo)
    tau = t_hi * 65536 + (t_lo - I16_MIN)
    tau = jnp.maximum(tau, INT_MIN + 1)
    tau_scr[...] = tau

    @pl.when(jnp.max(jnp.where(t_hi > I16_MIN, cnt_ge, 0)) > topk)
    def _():
        def count32(pred):
            def body(c, cnt):
                start = pl.multiple_of(c * kc, kc)
                hit = pred(key_scr[pl.ds(start, kc), :], start)
                return cnt + jnp.sum(hit.astype(I32).reshape(kc // 8, 8, tq), axis=0)
            cnt8 = lax.fori_loop(0, n_chunks, body, jnp.zeros((8, tq), I32))
            return jnp.sum(cnt8, axis=0, keepdims=True)

        cnt_gt = count32(lambda key, start: key > tau)

        def cut_body(b, cut):
            cand = cut + lax.shift_left(jnp.int32(1), idx_bits - 1 - b)

            def pred(key, start):
                kidx = start + lax.broadcasted_iota(I32, (kc, tq), 0)
                return (key == tau) & (kidx < cand)

            return jnp.where(cnt_gt + count32(pred) < topk, cand, cut)

        cut = lax.fori_loop(0, idx_bits, cut_body, jnp.zeros((1, tq), I32))

        def drop_body(c, _):
            start = pl.multiple_of(c * kc, kc)
            key = key_scr[pl.ds(start, kc), :]
            kidx = start + lax.broadcasted_iota(I32, (kc, tq), 0)
            key_scr[pl.ds(start, kc), :] = jnp.where((key == tau) & (kidx > cut), INT_MIN, key)
            return 0

        lax.fori_loop(0, n_chunks, drop_body, 0)

    m_scr[...] = jnp.full(m_scr.shape, NEG_BIG, F32)
    l_scr[...] = jnp.zeros(l_scr.shape, F32)
    acc_scr[...] = jnp.zeros(acc_scr.shape, F32)
    c2 = scale * LOG2E

    def logits(c):
        kd = kd_ref[0, pl.ds(pl.multiple_of(c * kc, kc), kc), :]
        return lax.dot_general(kd, qa_scr[...], NT_DIMS, preferred_element_type=F32)

    def weighted_values(c):
        vt = vt_ref[0, :, pl.ds(pl.multiple_of(c * kc, kc), kc)]
        acc_scr[...] = alpha_scr[...] * acc_scr[...] + jnp.dot(
            vt, p_scr[...], preferred_element_type=F32)

    s_scr[...] = logits(0)
    p_scr[...] = jnp.zeros(p_scr.shape, BF16)
    alpha_scr[...] = jnp.ones(alpha_scr.shape, F32)

    def att_body(c, _):
        weighted_values(jnp.maximum(c - 1, 0))
        nxt = logits(jnp.minimum(c + 1, n_chunks - 1))
        sel = key_scr[pl.ds(pl.multiple_of(c * kc, kc), kc), :] >= tau_scr[...]
        s = s_scr[...]
        s = jnp.concatenate([jnp.where(sel, s[:, h * tq:(h + 1) * tq], NEG_BIG)
                             for h in range(n_dsa)], axis=1)
        m_old = m_scr[...]
        m_new = jnp.maximum(m_old, jnp.max(s, axis=0, keepdims=True))
        alpha = jnp.exp2((m_old - m_new) * c2)
        p = jnp.exp2((s - m_new) * c2)
        l_scr[...] = alpha * l_scr[...] + jnp.sum(p, axis=0, keepdims=True)
        m_scr[...] = m_new
        alpha_scr[...] = alpha
        p_scr[...] = p.astype(BF16)
        s_scr[...] = nxt
        return 0

    lax.fori_loop(0, n_chunks, att_body, 0)
    weighted_values(n_chunks - 1)

    out_t = acc_scr[...] / l_scr[...]
    for h in range(n_dsa):
        o_ref[0, :, h * HEAD_DIM:(h + 1) * HEAD_DIM] = out_t[:, h * tq:(h + 1) * tq].T


def _dsa(proj3, v_t, w_t, batch, seq, n_dsa, scale, tq=128, kc=256):
    width = n_dsa * HEAD_DIM
    topk = min(TOPK_MAX, seq // 4)
    idx_bits = int(np.ceil(np.log2(seq)))
    kern = functools.partial(_dsa_kernel, scale=scale, topk=topk, kc=kc, idx_bits=idx_bits)
    assert width == 1024 and IDX_HEADS * IDX_DIM == 1024
    misc = 5 * 1024 // LANES
    return pl.pallas_call(
        kern,
        out_shape=jax.ShapeDtypeStruct((batch, seq, width), F32),
        grid=(batch, seq // tq),
        in_specs=[
            pl.BlockSpec((1, tq, width), lambda b, i: (b, i, 3)),
            pl.BlockSpec((1, tq, IDX_HEADS * IDX_DIM), lambda b, i: (b, i, 4)),
            pl.BlockSpec((1, seq, HEAD_DIM), lambda b, i: (b, 0, misc)),
            pl.BlockSpec((1, HEAD_DIM, seq), lambda b, i: (b, 0, 0)),
            pl.BlockSpec((1, seq, LANES), lambda b, i: (b, 0, misc + 2)),
            pl.BlockSpec((1, IDX_HEADS, tq), lambda b, i: (b, 0, i)),
        ],
        out_specs=pl.BlockSpec((1, tq, width), lambda b, i: (b, i, 0)),
        scratch_shapes=[
            pltpu.VMEM((IDX_HEADS * tq, LANES), BF16),
            pltpu.VMEM((n_dsa * tq, HEAD_DIM), BF16),
            pltpu.VMEM((seq, tq), I32),
            pltpu.VMEM((seq + kc, tq), I16),
            pltpu.VMEM((seq + kc, tq), I16),
            pltpu.VMEM((HEAD_DIM, n_dsa * tq), F32),
            pltpu.VMEM((1, n_dsa * tq), F32),
            pltpu.VMEM((1, n_dsa * tq), F32),
            pltpu.VMEM((1, tq), I32),
            pltpu.VMEM((kc, IDX_HEADS * tq), F32),
            pltpu.VMEM((kc, n_dsa * tq), F32),
            pltpu.VMEM((kc, n_dsa * tq), BF16),
            pltpu.VMEM((1, n_dsa * tq), F32),
        ],
        compiler_params=_cparams(("arbitrary", "arbitrary")),
        name="dsa",
    )(proj3, proj3, proj3, v_t, proj3, w_t)


def _rms(v, g):
    return v * lax.rsqrt(jnp.mean(v * v, axis=-1, keepdims=True) + EPS) * g


def _outp_kernel(osb_ref, ods_ref, x_ref, mod_ref, gsb_ref, gds_ref, g2_ref, w_ref,
                 x1_ref, h2_ref):
    half = osb_ref.shape[1]
    a = _rms(osb_ref[...], gsb_ref[...]).astype(BF16)
    b = _rms(ods_ref[...], gds_ref[...]).astype(BF16)
    y = (jnp.dot(a, w_ref[0:half, :], preferred_element_type=F32)
         + jnp.dot(b, w_ref[half:, :], preferred_element_type=F32))
    x1 = x_ref[...] + mod_ref[0, 2:3, :] * y
    x1_ref[...] = x1
    h2 = _rms(x1, g2_ref[...]) * (1.0 + mod_ref[0, 4:5, :]) + mod_ref[0, 3:4, :]
    h2_ref[...] = h2.astype(BF16)


def _outp(o_sb, o_ds, x2d, mod3, g_sb, g_ds, g2, w_out, seq, tm=512):
    rows, d = x2d.shape
    half = o_sb.shape[1]
    tiles_per_batch = seq // tm
    return pl.pallas_call(
        _outp_kernel,
        out_shape=(jax.ShapeDtypeStruct((rows, d), F32), jax.ShapeDtypeStruct((rows, d), BF16)),
        grid=(rows // tm,),
        in_specs=[
            pl.BlockSpec((tm, half), lambda i: (i, 0)),
            pl.BlockSpec((tm, half), lambda i: (i, 0)),
            pl.BlockSpec((tm, d), lambda i: (i, 0)),
            pl.BlockSpec((1, 6, d), lambda i: (i // tiles_per_batch, 0, 0)),
            pl.BlockSpec((1, half), lambda i: (0, 0)),
            pl.BlockSpec((1, half), lambda i: (0, 0)),
            pl.BlockSpec((1, d), lambda i: (0, 0)),
            pl.BlockSpec((2 * half, d), lambda i: (0, 0)),
        ],
        out_specs=(pl.BlockSpec((tm, d), lambda i: (i, 0)), pl.BlockSpec((tm, d), lambda i: (i, 0))),
        compiler_params=_cparams(("arbitrary",)),
        name="outp",
    )(o_sb, o_ds, x2d, mod3, g_sb, g_ds, g2, w_out)


def _ffn_kernel(h_ref, halo_ref, x1_ref, mod_ref, wg_ref, wv_ref, cwg_ref, cwv_ref,
                cbg_ref, cbv_ref, wd_ref, gf_ref, o_ref, ug_scr, uv_scr, *, tiles_per_batch,
                final_norm):
    i = pl.program_id(0)
    j = pl.program_id(1)
    tm = h_ref.shape[0]
    pad = halo_ref.shape[0]
    h = h_ref[...]
    halo = jnp.where(i % tiles_per_batch == 0, jnp.zeros_like(halo_ref[...]), halo_ref[...])

    def conv(w_ref, cw_ref, cb_ref, scr):
        scr[pad:, :] = jnp.dot(h, w_ref[...], preferred_element_type=F32)
        scr[0:pad, :] = jnp.dot(halo, w_ref[...], preferred_element_type=F32)
        uc = cb_ref[...] + scr[pad:, :] * cw_ref[CONV_WIDTH - 1:CONV_WIDTH, :]
        for tap in range(CONV_WIDTH - 1):
            back = CONV_WIDTH - 1 - tap
            uc = uc + scr[pad - back:pad - back + tm, :] * cw_ref[tap:tap + 1, :]
        return uc

    gate = conv(wg_ref, cwg_ref, cbg_ref, ug_scr)
    val = conv(wv_ref, cwv_ref, cbv_ref, uv_scr)
    a = (gate * jax.nn.sigmoid(gate) * val).astype(BF16)
    part = jnp.dot(a, wd_ref[...], preferred_element_type=F32)

    @pl.when(j == 0)
    def _():
        o_ref[...] = part

    @pl.when(j > 0)
    def _():
        o_ref[...] += part

    @pl.when(j == pl.num_programs(1) - 1)
    def _():
        x2 = x1_ref[...] + mod_ref[0, 5:6, :] * o_ref[...]
        o_ref[...] = _rms(x2, gf_ref[...]) if final_norm else x2


def _ffn(h2, x1, mod3, w_up_g, w_up_v, cw_g, cw_v, cb_g, cb_v, w_dn, g_f, seq, final_norm,
         tm=1024, tn=512):
    rows, d = x1.shape
    dff = w_up_g.shape[1]
    tiles_per_batch = seq // tm
    pad = 16
    kern = functools.partial(_ffn_kernel, tiles_per_batch=tiles_per_batch, final_norm=final_norm)
    return pl.pallas_call(
        kern,
        out_shape=jax.ShapeDtypeStruct((rows, d), F32),
        grid=(rows // tm, dff // tn),
        in_specs=[
            pl.BlockSpec((tm, d), lambda i, j: (i, 0), pipeline_mode=pl.Buffered(1)),
            pl.BlockSpec((pad, d), lambda i, j: (jnp.maximum(i * (tm // pad) - 1, 0), 0)),
            pl.BlockSpec((tm, d), lambda i, j: (i, 0), pipeline_mode=pl.Buffered(1)),
            pl.BlockSpec((1, 6, d), lambda i, j: (i // tiles_per_batch, 0, 0)),
            pl.BlockSpec((d, tn), lambda i, j: (0, j)),
            pl.BlockSpec((d, tn), lambda i, j: (0, j)),
            pl.BlockSpec((CONV_WIDTH, tn), lambda i, j: (0, j)),
            pl.BlockSpec((CONV_WIDTH, tn), lambda i, j: (0, j)),
            pl.BlockSpec((1, tn), lambda i, j: (0, j)),
            pl.BlockSpec((1, tn), lambda i, j: (0, j)),
            pl.BlockSpec((tn, d), lambda i, j: (j, 0)),
            pl.BlockSpec((1, d), lambda i, j: (0, 0)),
        ],
        out_specs=pl.BlockSpec((tm, d), lambda i, j: (i, 0), pipeline_mode=pl.Buffered(1)),
        scratch_shapes=[pltpu.VMEM((tm + pad, tn), F32), pltpu.VMEM((tm + pad, tn), F32)],
        compiler_params=_cparams(("arbitrary", "arbitrary")),
        name="ffn",
    )(h2, h2, x1, mod3, w_up_g, w_up_v, cw_g, cw_v, cb_g, cb_v, w_dn, g_f)


def _pad_cols(a, n):
    return jnp.pad(a, ((0, 0), (0, n - a.shape[1])))


def _regroup_w_in(w, d_model):
    sbw = d_model // 2
    p = np.cumsum([0, sbw, sbw, sbw, sbw, HEAD_DIM, HEAD_DIM, IDX_HEADS * IDX_DIM, IDX_DIM, IDX_HEADS])
    q_sb, k_sb, v_sb, q_ds, k_ds, v_ds, q_ix, k_ix, w_ix = [w[:, p[n]:p[n + 1]] for n in range(9)]
    return jnp.concatenate(
        [q_sb, k_sb, v_sb, q_ds, q_ix, k_ds, v_ds, k_ix, k_ix, _pad_cols(w_ix, LANES)], axis=1)


def kernel(x, c, positions, w_ada, b_ada, norm1_g, w_in, sb_norm_g, dsa_norm_g, w_out, norm2_g,
           w_up, conv_w, conv_b, w_down, final_norm_g):
    batch, seq, d = x.shape
    depth = w_ada.shape[0]
    n_sb = (d // HEAD_DIM) // 2
    n_dsa = d // HEAD_DIM - n_sb
    d_ff = w_down.shape[1]
    ff_pad = -(-d_ff // 512) * 512
    idx_scale = (IDX_HEADS ** -0.5) * (IDX_DIM ** -0.5)
    scale = HEAD_DIM ** -0.5
    rows = batch * seq

    half128 = HEAD_DIM // ROPE_FRACTION // 2
    half64 = IDX_DIM // ROPE_FRACTION // 2
    f128 = ROPE_THETA ** (-jnp.arange(half128, dtype=F32) / half128)
    f64 = ROPE_THETA ** (-jnp.arange(half64, dtype=F32) / half64)
    invf = jnp.zeros((8, LANES), F32)
    invf = invf.at[0].set(jnp.tile(f128, LANES // half128)).at[1].set(jnp.tile(f64, LANES // half64))

    tri_np = np.zeros((2 * SUB, 2 * SUB), np.float32)
    later = (np.arange(SUB)[:, None] > np.arange(SUB)[None, :]).astype(np.float32)
    tri_np[:SUB, :SUB] = later
    tri_np[SUB:, :SUB] = later
    tri_np[:, SUB:] = 1.0
    tri = jnp.asarray(tri_np, BF16)

    pos_col = positions.reshape(rows, 1)
    c_pad = jnp.pad(c, ((0, 16 - batch), (0, 0)))
    x2d = x.reshape(rows, d)

    for l in range(depth):
        mod = _ada(c_pad, w_ada[l], b_ada[l][None, :])[:batch]
        mod3 = mod.reshape(batch, 6, d)

        w_in_p = _regroup_w_in(w_in[l], d).astype(BF16)
        proj, v_t, w_t = _proj(x2d, mod3, norm1_g[l][None, :], pos_col, invf, w_in_p,
                               batch, seq, idx_scale)
        proj3 = proj.reshape(batch, seq, proj.shape[1])

        o_sb = _sb(proj3, tri, batch, seq, n_sb, scale)
        o_ds = _dsa(proj3, v_t, w_t, batch, seq, n_dsa, scale)

        x1, h2 = _outp(o_sb.reshape(rows, -1), o_ds.reshape(rows, -1), x2d, mod3,
                       sb_norm_g[l][None, :], dsa_norm_g[l][None, :], norm2_g[l][None, :],
                       w_out[l].astype(BF16), seq)

        wu = w_up[l]
        w_up_g = _pad_cols(wu[:, :d_ff], ff_pad).astype(BF16)
        w_up_v = _pad_cols(wu[:, d_ff:], ff_pad).astype(BF16)
        cw_g = _pad_cols(conv_w[l][:, :d_ff], ff_pad)
        cw_v = _pad_cols(conv_w[l][:, d_ff:], ff_pad)
        cb_g = _pad_cols(conv_b[l][None, :d_ff], ff_pad)
        cb_v = _pad_cols(conv_b[l][None, d_ff:], ff_pad)
        w_dn = jnp.pad(w_down[l], ((0, ff_pad - d_ff), (0, 0))).astype(BF16)
        x2d = _ffn(h2, x1, mod3, w_up_g, w_up_v, cw_g, cw_v, cb_g, cb_v, w_dn,
                   final_norm_g[None, :], seq, final_norm=(l == depth - 1))
    return x2d.reshape(batch, seq, d)
```

```python
import functools

import jax
import jax.numpy as jnp
import numpy as np
from jax import lax
from jax.experimental import pallas as pl
from jax.experimental.pallas import tpu as pltpu

F32 = jnp.float32
BF16 = jnp.bfloat16
I32 = jnp.int32

HEAD_DIM = 128
IDX_HEADS = 16
IDX_DIM = 64
TOPK_MAX = 256
ROPE_THETA = 500000.0
ROPE_FRACTION = 4
CONV_WIDTH = 3
EPS = 1e-6
NEG_BIG = -1e30
INT_MIN = -(2 ** 31)
LOG2E = 1.4426950408889634

LANES = 128
VMEM_LIMIT = 56 * 1024 * 1024

NT_DIMS = (((1,), (1,)), ((), ()))


def _cparams(sem):
    return pltpu.CompilerParams(dimension_semantics=sem, vmem_limit_bytes=VMEM_LIMIT)


def _ada_kernel(c_ref, w_ref, b_ref, o_ref):
    c = c_ref[...]
    s = (c * jax.nn.sigmoid(c)).astype(BF16)
    o_ref[...] = jnp.dot(s, w_ref[...].astype(BF16), preferred_element_type=F32) + b_ref[...]


def _ada(c_pad, w_ada, b_ada, tn=1024):
    m, d = c_pad.shape
    n = w_ada.shape[1]
    return pl.pallas_call(
        _ada_kernel,
        out_shape=jax.ShapeDtypeStruct((m, n), F32),
        grid=(n // tn,),
        in_specs=[
            pl.BlockSpec((m, d), lambda j: (0, 0)),
            pl.BlockSpec((d, tn), lambda j: (0, j)),
            pl.BlockSpec((1, tn), lambda j: (0, j)),
        ],
        out_specs=pl.BlockSpec((m, tn), lambda j: (0, j)),
        compiler_params=_cparams(("arbitrary",)),
        name="ada",
    )(c_pad, w_ada, b_ada)


def _rope(xb, c, sn, sp, shift):
    return (xb * c + pltpu.roll(xb, LANES - shift, axis=1) * sn
            + pltpu.roll(xb, shift, axis=1) * sp)


def _proj_kernel(x_ref, mod_ref, g_ref, pos_ref, invf_ref, w_ref,
                 o_ref, vt_ref, wt_ref, h_scr, tab_scr, *, idx_scale):
    j = pl.program_id(1)
    tn = o_ref.shape[1]
    nsub = tn // LANES

    @pl.when(j == 0)
    def _():
        x = x_ref[...]
        ms = jnp.mean(x * x, axis=-1, keepdims=True)
        y = x * lax.rsqrt(ms + EPS) * g_ref[...]
        h = y * (1.0 + mod_ref[0, 1:2, :]) + mod_ref[0, 0:1, :]
        h_scr[...] = h.astype(BF16)
        pos = pos_ref[...].astype(F32)
        lane = lax.broadcasted_iota(I32, (1, LANES), 1)
        a128 = pos * invf_ref[0:1, :]
        c128, s128 = jnp.cos(a128), jnp.sin(a128)
        tab_scr[0] = jnp.where(lane < 32, c128, 1.0)
        tab_scr[1] = jnp.where(lane < 16, -s128, 0.0)
        tab_scr[2] = jnp.where((lane >= 16) & (lane < 32), s128, 0.0)
        l64 = lane & 63
        a64 = pos * invf_ref[1:2, :]
        c64, s64 = jnp.cos(a64), jnp.sin(a64)
        tab_scr[3] = jnp.where(l64 < 16, c64, 1.0)
        tab_scr[4] = jnp.where(l64 < 8, -s64, 0.0)
        tab_scr[5] = jnp.where((l64 >= 8) & (l64 < 16), s64, 0.0)

    acc = jnp.dot(h_scr[...], w_ref[...], preferred_element_type=F32)

    def sub(s):
        return acc[:, s * LANES:(s + 1) * LANES]

    def rope128(xb):
        return _rope(xb, tab_scr[0], tab_scr[1], tab_scr[2], 16)

    def rope64(xb):
        return _rope(xb, tab_scr[3], tab_scr[4], tab_scr[5], 8)

    def store(s, val):
        o_ref[:, s * LANES:(s + 1) * LANES] = val.astype(BF16)

    @pl.when(j < 6)
    def _():
        o_ref[...] = acc.astype(BF16)

    @pl.when((j == 6) | (j == 7))
    def _():
        for s in range(nsub):
            store(s, rope128(sub(s)))

    @pl.when((j == 8) | (j == 9))
    def _():
        for s in range(nsub):
            store(s, rope64(sub(s)))

    @pl.when(j == 10)
    def _():
        store(0, rope128(sub(0)))
        v = sub(1)
        store(1, v)
        vt_ref[0] = v.T.astype(BF16)
        store(2, rope64(sub(2)))
        w = sub(3)
        store(3, w)
        wt_ref[0] = (w * idx_scale).T[0:IDX_HEADS, :]


def _proj(x2d, mod3, g1, pos_col, invf, w_in_p, batch, seq, idx_scale, tm=1024, tn=512):
    rows, d = x2d.shape
    ncols = w_in_p.shape[1]
    tiles_per_batch = seq // tm
    kern = functools.partial(_proj_kernel, idx_scale=idx_scale)
    return pl.pallas_call(
        kern,
        out_shape=(
            jax.ShapeDtypeStruct((rows, ncols), BF16),
            jax.ShapeDtypeStruct((batch, HEAD_DIM, seq), BF16),
            jax.ShapeDtypeStruct((batch, IDX_HEADS, seq), F32),
        ),
        grid=(rows // tm, ncols // tn),
        in_specs=[
            pl.BlockSpec((tm, d), lambda i, j: (i, 0)),
            pl.BlockSpec((1, 6, d), lambda i, j: (i // tiles_per_batch, 0, 0)),
            pl.BlockSpec((1, d), lambda i, j: (0, 0)),
            pl.BlockSpec((tm, 1), lambda i, j: (i, 0)),
            pl.BlockSpec((8, LANES), lambda i, j: (0, 0)),
            pl.BlockSpec((d, tn), lambda i, j: (0, j)),
        ],
        out_specs=(
            pl.BlockSpec((tm, tn), lambda i, j: (i, j)),
            pl.BlockSpec((1, HEAD_DIM, tm),
                         lambda i, j: (i // tiles_per_batch, 0, i % tiles_per_batch)),
            pl.BlockSpec((1, IDX_HEADS, tm),
                         lambda i, j: (i // tiles_per_batch, 0, i % tiles_per_batch)),
        ),
        scratch_shapes=[
            pltpu.VMEM((tm, d), BF16),
            pltpu.VMEM((6, tm, LANES), F32),
        ],
        compiler_params=_cparams(("arbitrary", "arbitrary")),
        name="proj",
    )(x2d, mod3, g1, pos_col, invf, w_in_p)


SUB = 128


def _sb_kernel(q_ref, k_ref, v_ref, tri_ref, o_ref, carry_scr, acc_scr, *, scale, kc):
    i = pl.program_id(2)
    tq = q_ref.shape[1]
    nsub = kc // SUB
    n_diag = tq // kc
    q = q_ref[0]
    tri = tri_ref[...]
    t_row = i * tq + lax.broadcasted_iota(I32, (tq, kc), 0)

    def chunk(j, diag):
        start = pl.multiple_of(j * kc, kc)
        k = k_ref[0, pl.ds(start, kc), :]
        v = v_ref[0, pl.ds(start, kc), :]
        zn = lax.dot_general(q, k, NT_DIMS, preferred_element_type=F32) * (-scale)
        sp = jnp.log(1.0 + jnp.exp2(jnp.abs(zn) * (-LOG2E)))
        lneg = jnp.minimum(zn, 0.0) - sp
        lpos = lneg - zn
        if diag:
            strict = (start + lax.broadcasted_iota(I32, (tq, kc), 1)) < t_row
            lneg = jnp.where(strict, lneg, 0.0)
        hi = lneg.astype(BF16)
        lo = (lneg - hi.astype(F32)).astype(BF16)
        carry = carry_scr[...]
        parts = [None] * nsub
        for s in reversed(range(nsub)):
            cols = slice(s * SUB, (s + 1) * SUB)
            r = jnp.dot(jnp.concatenate([hi[:, cols], lo[:, cols]], axis=1), tri,
                        preferred_element_type=F32)
            parts[s] = lpos[:, cols] + r[:, :SUB] + carry
            carry = carry + r[:, SUB:]
        carry_scr[...] = carry
        w = jnp.exp(jnp.concatenate(parts, axis=1))
        if diag:
            w = jnp.where(strict, w, 0.0)
        acc_scr[...] += jnp.dot(w.astype(BF16), v, preferred_element_type=F32)

    carry_scr[...] = jnp.zeros(carry_scr.shape, F32)
    acc_scr[...] = jnp.zeros(acc_scr.shape, F32)
    first = i * n_diag
    for n in range(n_diag):
        chunk(first + n_diag - 1 - n, True)

    def body(n, _):
        for u in range(n_diag):
            chunk(first - 1 - n_diag * n - u, False)
        return 0

    lax.fori_loop(0, i, body, 0)
    o_ref[0] = acc_scr[...]


def _sb(proj3, tri, batch, seq, n_heads, scale, tq=512, kc=256):
    kern = functools.partial(_sb_kernel, scale=scale, kc=kc)
    return pl.pallas_call(
        kern,
        out_shape=jax.ShapeDtypeStruct((batch, seq, n_heads * HEAD_DIM), F32),
        grid=(batch, n_heads, seq // tq),
        in_specs=[
            pl.BlockSpec((1, tq, HEAD_DIM), lambda b, h, i: (b, i, h)),
            pl.BlockSpec((1, seq, HEAD_DIM), lambda b, h, i: (b, 0, n_heads + h)),
            pl.BlockSpec((1, seq, HEAD_DIM), lambda b, h, i: (b, 0, 2 * n_heads + h)),
            pl.BlockSpec((2 * SUB, 2 * SUB), lambda b, h, i: (0, 0)),
        ],
        out_specs=pl.BlockSpec((1, tq, HEAD_DIM), lambda b, h, i: (b, i, h)),
        scratch_shapes=[pltpu.VMEM((tq, SUB), F32), pltpu.VMEM((tq, HEAD_DIM), F32)],
        compiler_params=_cparams(("arbitrary", "arbitrary", "arbitrary")),
        name="sb",
    )(proj3, proj3, proj3, tri)


def _dsa_kernel(qd_ref, qi_ref, kd_ref, vt_ref, ki_ref, wt_ref, o_ref,
                qm_scr, qa_scr, key_scr, acc_scr, m_scr, l_scr, tau_scr, s_scr, p_scr, alpha_scr,
                *, scale, topk, kc, idx_bits):
    i = pl.program_id(1)
    tq = qd_ref.shape[1]
    n_dsa = qd_ref.shape[2] // HEAD_DIM
    n_chunks = ((i + 1) * tq + kc - 1) // kc
    ks = 2 * kc
    n_trips = (n_chunks + 1) // 2
    t_row = i * tq + lax.broadcasted_iota(I32, (1, tq), 1)

    lane = lax.broadcasted_iota(I32, (tq, LANES), 1)
    for h in range(IDX_HEADS):
        blk = qi_ref[0, :, (h // 2) * LANES:(h // 2 + 1) * LANES]
        keep = (lane < IDX_DIM) if h % 2 == 0 else (lane >= IDX_DIM)
        qm_scr[h * tq:(h + 1) * tq, :] = jnp.where(keep, blk, jnp.zeros_like(blk))
    for h in range(n_dsa):
        qa_scr[h * tq:(h + 1) * tq, :] = qd_ref[0, :, h * HEAD_DIM:(h + 1) * HEAD_DIM]

    def score_chunk(c, masked):
        start = pl.multiple_of(c * kc, kc)
        kch = ki_ref[0, pl.ds(start, kc), :]
        r = lax.dot_general(kch, qm_scr[...], NT_DIMS, preferred_element_type=F32)
        sc = jnp.zeros((kc, tq), F32)
        for h in range(IDX_HEADS):
            sc = sc + wt_ref[0, h:h + 1, :] * jnp.maximum(r[:, h * tq:(h + 1) * tq], 0.0)
        bits = pltpu.bitcast(sc, I32)
        key = jnp.where(bits >= 0, bits, bits ^ 0x7FFFFFFF)
        if masked:
            kidx = start + lax.broadcasted_iota(I32, (kc, tq), 0)
            key = jnp.where(kidx <= t_row, key, INT_MIN)
        key_scr[pl.ds(start, kc), :] = key

    def score_body(c, _):
        score_chunk(c, False)
        return 0

    lax.fori_loop(0, n_chunks - 1, score_body, 0)
    score_chunk(n_chunks - 1, True)
    key_scr[pl.ds(pl.multiple_of(n_chunks * kc, kc), kc), :] = jnp.full((kc, tq), INT_MIN, I32)

    def count(pred):
        def body(t, cnt):
            start = pl.multiple_of(t * ks, ks)
            hit = jnp.where(pred(key_scr[pl.ds(start, ks), :], start), 1, 0)
            parts = [hit[r * 8:(r + 1) * 8, :] for r in range(ks // 8)]
            while len(parts) > 1:
                parts = [a + b for a, b in zip(parts[0::2], parts[1::2])]
            return cnt + parts[0]

        cnt8 = lax.fori_loop(0, n_trips, body, jnp.zeros((8, tq), I32))
        return jnp.sum(cnt8, axis=0, keepdims=True)

    def tau_body(b, tau):
        cand = tau + lax.shift_left(jnp.int32(1), 31 - b)
        return jnp.where(count(lambda key, start: key >= cand) >= topk, cand, tau)

    tau = lax.fori_loop(0, 32, tau_body, jnp.full((1, tq), INT_MIN, I32))
    tau = jnp.maximum(tau, INT_MIN + 1)
    tau_scr[...] = tau
    cnt_ge = count(lambda key, start: key >= tau)

    @pl.when(jnp.max(cnt_ge) > topk)
    def _():
        cnt_gt = count(lambda key, start: key > tau)

        def cut_body(b, cut):
            cand = cut + lax.shift_left(jnp.int32(1), idx_bits - 1 - b)

            def pred(key, start):
                kidx = start + lax.broadcasted_iota(I32, (ks, tq), 0)
                return (key == tau) & (kidx < cand)

            return jnp.where(cnt_gt + count(pred) < topk, cand, cut)

        cut = lax.fori_loop(0, idx_bits, cut_body, jnp.zeros((1, tq), I32))

        def drop_body(t, _):
            start = pl.multiple_of(t * ks, ks)
            key = key_scr[pl.ds(start, ks), :]
            kidx = start + lax.broadcasted_iota(I32, (ks, tq), 0)
            key_scr[pl.ds(start, ks), :] = jnp.where((key == tau) & (kidx > cut), INT_MIN, key)
            return 0

        lax.fori_loop(0, n_trips, drop_body, 0)

    m_scr[...] = jnp.full(m_scr.shape, NEG_BIG, F32)
    l_scr[...] = jnp.zeros(l_scr.shape, F32)
    acc_scr[...] = jnp.zeros(acc_scr.shape, F32)
    c2 = scale * LOG2E

    def logits(c):
        kd = kd_ref[0, pl.ds(pl.multiple_of(c * kc, kc), kc), :]
        return lax.dot_general(kd, qa_scr[...], NT_DIMS, preferred_element_type=F32)

    def weighted_values(c):
        vt = vt_ref[0, :, pl.ds(pl.multiple_of(c * kc, kc), kc)]
        acc_scr[...] = alpha_scr[...] * acc_scr[...] + jnp.dot(
            vt, p_scr[...], preferred_element_type=F32)

    s_scr[...] = logits(0)
    p_scr[...] = jnp.zeros(p_scr.shape, BF16)
    alpha_scr[...] = jnp.ones(alpha_scr.shape, F32)

    def att_body(c, _):
        weighted_values(jnp.maximum(c - 1, 0))
        nxt = logits(jnp.minimum(c + 1, n_chunks - 1))
        sel = key_scr[pl.ds(pl.multiple_of(c * kc, kc), kc), :] >= tau_scr[...]
        s = s_scr[...]
        s = jnp.concatenate([jnp.where(sel, s[:, h * tq:(h + 1) * tq], NEG_BIG)
                             for h in range(n_dsa)], axis=1)
        m_old = m_scr[...]
        m_new = jnp.maximum(m_old, jnp.max(s, axis=0, keepdims=True))
        alpha = jnp.exp2((m_old - m_new) * c2)
        p = jnp.exp2((s - m_new) * c2)
        l_scr[...] = alpha * l_scr[...] + jnp.sum(p, axis=0, keepdims=True)
        m_scr[...] = m_new
        alpha_scr[...] = alpha
        p_scr[...] = p.astype(BF16)
        s_scr[...] = nxt
        return 0

    lax.fori_loop(0, n_chunks, att_body, 0)
    weighted_values(n_chunks - 1)

    out_t = acc_scr[...] / l_scr[...]
    for h in range(n_dsa):
        o_ref[0, :, h * HEAD_DIM:(h + 1) * HEAD_DIM] = out_t[:, h * tq:(h + 1) * tq].T


def _dsa(proj3, v_t, w_t, batch, seq, n_dsa, scale, tq=128, kc=256):
    width = n_dsa * HEAD_DIM
    topk = min(TOPK_MAX, seq // 4)
    idx_bits = int(np.ceil(np.log2(seq)))
    kern = functools.partial(_dsa_kernel, scale=scale, topk=topk, kc=kc, idx_bits=idx_bits)
    assert width == 1024 and IDX_HEADS * IDX_DIM == 1024
    misc = 5 * 1024 // LANES
    return pl.pallas_call(
        kern,
        out_shape=jax.ShapeDtypeStruct((batch, seq, width), F32),
        grid=(batch, seq // tq),
        in_specs=[
            pl.BlockSpec((1, tq, width), lambda b, i: (b, i, 3)),
            pl.BlockSpec((1, tq, IDX_HEADS * IDX_DIM), lambda b, i: (b, i, 4)),
            pl.BlockSpec((1, seq, HEAD_DIM), lambda b, i: (b, 0, misc)),
            pl.BlockSpec((1, HEAD_DIM, seq), lambda b, i: (b, 0, 0)),
            pl.BlockSpec((1, seq, LANES), lambda b, i: (b, 0, misc + 2)),
            pl.BlockSpec((1, IDX_HEADS, tq), lambda b, i: (b, 0, i)),
        ],
        out_specs=pl.BlockSpec((1, tq, width), lambda b, i: (b, i, 0)),
        scratch_shapes=[
            pltpu.VMEM((IDX_HEADS * tq, LANES), BF16),
            pltpu.VMEM((n_dsa * tq, HEAD_DIM), BF16),
            pltpu.VMEM((seq + kc, tq), I32),
            pltpu.VMEM((HEAD_DIM, n_dsa * tq), F32),
            pltpu.VMEM((1, n_dsa * tq), F32),
            pltpu.VMEM((1, n_dsa * tq), F32),
            pltpu.VMEM((1, tq), I32),
            pltpu.VMEM((kc, n_dsa * tq), F32),
            pltpu.VMEM((kc, n_dsa * tq), BF16),
            pltpu.VMEM((1, n_dsa * tq), F32),
        ],
        compiler_params=_cparams(("arbitrary", "arbitrary")),
        name="dsa",
    )(proj3, proj3, proj3, v_t, proj3, w_t)


def _rms(v, g):
    return v * lax.rsqrt(jnp.mean(v * v, axis=-1, keepdims=True) + EPS) * g


def _outp_kernel(osb_ref, ods_ref, x_ref, mod_ref, gsb_ref, gds_ref, g2_ref, w_ref,
                 x1_ref, h2_ref):
    half = osb_ref.shape[1]
    a = _rms(osb_ref[...], gsb_ref[...]).astype(BF16)
    b = _rms(ods_ref[...], gds_ref[...]).astype(BF16)
    y = (jnp.dot(a, w_ref[0:half, :], preferred_element_type=F32)
         + jnp.dot(b, w_ref[half:, :], preferred_element_type=F32))
    x1 = x_ref[...] + mod_ref[0, 2:3, :] * y
    x1_ref[...] = x1
    h2 = _rms(x1, g2_ref[...]) * (1.0 + mod_ref[0, 4:5, :]) + mod_ref[0, 3:4, :]
    h2_ref[...] = h2.astype(BF16)


def _outp(o_sb, o_ds, x2d, mod3, g_sb, g_ds, g2, w_out, seq, tm=512):
    rows, d = x2d.shape
    half = o_sb.shape[1]
    tiles_per_batch = seq // tm
    return pl.pallas_call(
        _outp_kernel,
        out_shape=(jax.ShapeDtypeStruct((rows, d), F32), jax.ShapeDtypeStruct((rows, d), BF16)),
        grid=(rows // tm,),
        in_specs=[
            pl.BlockSpec((tm, half), lambda i: (i, 0)),
            pl.BlockSpec((tm, half), lambda i: (i, 0)),
            pl.BlockSpec((tm, d), lambda i: (i, 0)),
            pl.BlockSpec((1, 6, d), lambda i: (i // tiles_per_batch, 0, 0)),
            pl.BlockSpec((1, half), lambda i: (0, 0)),
            pl.BlockSpec((1, half), lambda i: (0, 0)),
            pl.BlockSpec((1, d), lambda i: (0, 0)),
            pl.BlockSpec((2 * half, d), lambda i: (0, 0)),
        ],
        out_specs=(pl.BlockSpec((tm, d), lambda i: (i, 0)), pl.BlockSpec((tm, d), lambda i: (i, 0))),
        compiler_params=_cparams(("arbitrary",)),
        name="outp",
    )(o_sb, o_ds, x2d, mod3, g_sb, g_ds, g2, w_out)


def _ffn_kernel(h_ref, halo_ref, x1_ref, mod_ref, wg_ref, wv_ref, cwg_ref, cwv_ref,
                cbg_ref, cbv_ref, wd_ref, gf_ref, o_ref, ug_scr, uv_scr, *, tiles_per_batch,
                final_norm):
    i = pl.program_id(0)
    j = pl.program_id(1)
    tm = h_ref.shape[0]
    pad = halo_ref.shape[0]
    h = h_ref[...]
    halo = jnp.where(i % tiles_per_batch == 0, jnp.zeros_like(halo_ref[...]), halo_ref[...])

    def conv(w_ref, cw_ref, cb_ref, scr):
        scr[pad:, :] = jnp.dot(h, w_ref[...], preferred_element_type=F32)
        scr[0:pad, :] = jnp.dot(halo, w_ref[...], preferred_element_type=F32)
        uc = cb_ref[...] + scr[pad:, :] * cw_ref[CONV_WIDTH - 1:CONV_WIDTH, :]
        for tap in range(CONV_WIDTH - 1):
            back = CONV_WIDTH - 1 - tap
            uc = uc + scr[pad - back:pad - back + tm, :] * cw_ref[tap:tap + 1, :]
        return uc

    gate = conv(wg_ref, cwg_ref, cbg_ref, ug_scr)
    val = conv(wv_ref, cwv_ref, cbv_ref, uv_scr)
    a = (gate * jax.nn.sigmoid(gate) * val).astype(BF16)
    part = jnp.dot(a, wd_ref[...], preferred_element_type=F32)

    @pl.when(j == 0)
    def _():
        o_ref[...] = part

    @pl.when(j > 0)
    def _():
        o_ref[...] += part

    @pl.when(j == pl.num_programs(1) - 1)
    def _():
        x2 = x1_ref[...] + mod_ref[0, 5:6, :] * o_ref[...]
        o_ref[...] = _rms(x2, gf_ref[...]) if final_norm else x2


def _ffn(h2, x1, mod3, w_up_g, w_up_v, cw_g, cw_v, cb_g, cb_v, w_dn, g_f, seq, final_norm,
         tm=1024, tn=512):
    rows, d = x1.shape
    dff = w_up_g.shape[1]
    tiles_per_batch = seq // tm
    pad = 16
    kern = functools.partial(_ffn_kernel, tiles_per_batch=tiles_per_batch, final_norm=final_norm)
    return pl.pallas_call(
        kern,
        out_shape=jax.ShapeDtypeStruct((rows, d), F32),
        grid=(rows // tm, dff // tn),
        in_specs=[
            pl.BlockSpec((tm, d), lambda i, j: (i, 0), pipeline_mode=pl.Buffered(1)),
            pl.BlockSpec((pad, d), lambda i, j: (jnp.maximum(i * (tm // pad) - 1, 0), 0)),
            pl.BlockSpec((tm, d), lambda i, j: (i, 0), pipeline_mode=pl.Buffered(1)),
            pl.BlockSpec((1, 6, d), lambda i, j: (i // tiles_per_batch, 0, 0)),
            pl.BlockSpec((d, tn), lambda i, j: (0, j)),
            pl.BlockSpec((d, tn), lambda i, j: (0, j)),
            pl.BlockSpec((CONV_WIDTH, tn), lambda i, j: (0, j)),
            pl.BlockSpec((CONV_WIDTH, tn), lambda i, j: (0, j)),
            pl.BlockSpec((1, tn), lambda i, j: (0, j)),
            pl.BlockSpec((1, tn), lambda i, j: (0, j)),
            pl.BlockSpec((tn, d), lambda i, j: (j, 0)),
            pl.BlockSpec((1, d), lambda i, j: (0, 0)),
        ],
        out_specs=pl.BlockSpec((tm, d), lambda i, j: (i, 0), pipeline_mode=pl.Buffered(1)),
        scratch_shapes=[pltpu.VMEM((tm + pad, tn), F32), pltpu.VMEM((tm + pad, tn), F32)],
        compiler_params=_cparams(("arbitrary", "arbitrary")),
        name="ffn",
    )(h2, h2, x1, mod3, w_up_g, w_up_v, cw_g, cw_v, cb_g, cb_v, w_dn, g_f)


def _pad_cols(a, n):
    return jnp.pad(a, ((0, 0), (0, n - a.shape[1])))


def _regroup_w_in(w, d_model):
    sbw = d_model // 2
    p = np.cumsum([0, sbw, sbw, sbw, sbw, HEAD_DIM, HEAD_DIM, IDX_HEADS * IDX_DIM, IDX_DIM, IDX_HEADS])
    q_sb, k_sb, v_sb, q_ds, k_ds, v_ds, q_ix, k_ix, w_ix = [w[:, p[n]:p[n + 1]] for n in range(9)]
    return jnp.concatenate(
        [q_sb, k_sb, v_sb, q_ds, q_ix, k_ds, v_ds, k_ix, k_ix, _pad_cols(w_ix, LANES)], axis=1)


def kernel(x, c, positions, w_ada, b_ada, norm1_g, w_in, sb_norm_g, dsa_norm_g, w_out, norm2_g,
           w_up, conv_w, conv_b, w_down, final_norm_g):
    batch, seq, d = x.shape
    depth = w_ada.shape[0]
    n_sb = (d // HEAD_DIM) // 2
    n_dsa = d // HEAD_DIM - n_sb
    d_ff = w_down.shape[1]
    ff_pad = -(-d_ff // 512) * 512
    idx_scale = (IDX_HEADS ** -0.5) * (IDX_DIM ** -0.5)
    scale = HEAD_DIM ** -0.5
    rows = batch * seq

    half128 = HEAD_DIM // ROPE_FRACTION // 2
    half64 = IDX_DIM // ROPE_FRACTION // 2
    f128 = ROPE_THETA ** (-jnp.arange(half128, dtype=F32) / half128)
    f64 = ROPE_THETA ** (-jnp.arange(half64, dtype=F32) / half64)
    invf = jnp.zeros((8, LANES), F32)
    invf = invf.at[0].set(jnp.tile(f128, LANES // half128)).at[1].set(jnp.tile(f64, LANES // half64))

    tri_np = np.zeros((2 * SUB, 2 * SUB), np.float32)
    later = (np.arange(SUB)[:, None] > np.arange(SUB)[None, :]).astype(np.float32)
    tri_np[:SUB, :SUB] = later
    tri_np[SUB:, :SUB] = later
    tri_np[:, SUB:] = 1.0
    tri = jnp.asarray(tri_np, BF16)

    pos_col = positions.reshape(rows, 1)
    c_pad = jnp.pad(c, ((0, 16 - batch), (0, 0)))
    x2d = x.reshape(rows, d)

    for l in range(depth):
        mod = _ada(c_pad, w_ada[l], b_ada[l][None, :])[:batch]
        mod3 = mod.reshape(batch, 6, d)

        w_in_p = _regroup_w_in(w_in[l], d).astype(BF16)
        proj, v_t, w_t = _proj(x2d, mod3, norm1_g[l][None, :], pos_col, invf, w_in_p,
                               batch, seq, idx_scale)
        proj3 = proj.reshape(batch, seq, proj.shape[1])

        o_sb = _sb(proj3, tri, batch, seq, n_sb, scale)
        o_ds = _dsa(proj3, v_t, w_t, batch, seq, n_dsa, scale)

        x1, h2 = _outp(o_sb.reshape(rows, -1), o_ds.reshape(rows, -1), x2d, mod3,
                       sb_norm_g[l][None, :], dsa_norm_g[l][None, :], norm2_g[l][None, :],
                       w_out[l].astype(BF16), seq)

        wu = w_up[l]
        w_up_g = _pad_cols(wu[:, :d_ff], ff_pad).astype(BF16)
        w_up_v = _pad_cols(wu[:, d_ff:], ff_pad).astype(BF16)
        cw_g = _pad_cols(conv_w[l][:, :d_ff], ff_pad)
        cw_v = _pad_cols(conv_w[l][:, d_ff:], ff_pad)
        cb_g = _pad_cols(conv_b[l][None, :d_ff], ff_pad)
        cb_v = _pad_cols(conv_b[l][None, d_ff:], ff_pad)
        w_dn = jnp.pad(w_down[l], ((0, ff_pad - d_ff), (0, 0))).astype(BF16)
        x2d = _ffn(h2, x1, mod3, w_up_g, w_up_v, cw_g, cw_v, cb_g, cb_v, w_dn,
                   final_norm_g[None, :], seq, final_norm=(l == depth - 1))
    return x2d.reshape(batch, seq, d)
```

```python
import functools

import jax
import jax.numpy as jnp
import numpy as np
from jax import lax
from jax.experimental import pallas as pl
from jax.experimental.pallas import tpu as pltpu

F32 = jnp.float32
BF16 = jnp.bfloat16
I32 = jnp.int32

HEAD_DIM = 128
IDX_HEADS = 16
IDX_DIM = 64
TOPK_MAX = 256
ROPE_THETA = 500000.0
ROPE_FRACTION = 4
CONV_WIDTH = 3
EPS = 1e-6
NEG_BIG = -1e30
INT_MIN = -(2 ** 31)
LOG2E = 1.4426950408889634

LANES = 128
VMEM_LIMIT = 56 * 1024 * 1024

NT_DIMS = (((1,), (1,)), ((), ()))


def _cparams(sem):
    return pltpu.CompilerParams(dimension_semantics=sem, vmem_limit_bytes=VMEM_LIMIT)


def _ada_kernel(c_ref, w_ref, b_ref, o_ref):
    c = c_ref[...]
    s = (c * jax.nn.sigmoid(c)).astype(BF16)
    o_ref[...] = jnp.dot(s, w_ref[...].astype(BF16), preferred_element_type=F32) + b_ref[...]


def _ada(c_pad, w_ada, b_ada, tn=1024):
    m, d = c_pad.shape
    n = w_ada.shape[1]
    return pl.pallas_call(
        _ada_kernel,
        out_shape=jax.ShapeDtypeStruct((m, n), F32),
        grid=(n // tn,),
        in_specs=[
            pl.BlockSpec((m, d), lambda j: (0, 0)),
            pl.BlockSpec((d, tn), lambda j: (0, j)),
            pl.BlockSpec((1, tn), lambda j: (0, j)),
        ],
        out_specs=pl.BlockSpec((m, tn), lambda j: (0, j)),
        compiler_params=_cparams(("arbitrary",)),
        name="ada",
    )(c_pad, w_ada, b_ada)


def _rope(xb, c, sn, sp, shift):
    return (xb * c + pltpu.roll(xb, LANES - shift, axis=1) * sn
            + pltpu.roll(xb, shift, axis=1) * sp)


def _proj_kernel(x_ref, mod_ref, g_ref, pos_ref, invf_ref, w_ref,
                 o_ref, vt_ref, wt_ref, h_scr, tab_scr, *, idx_scale):
    j = pl.program_id(1)
    tn = o_ref.shape[1]
    nsub = tn // LANES

    @pl.when(j == 0)
    def _():
        x = x_ref[...]
        ms = jnp.mean(x * x, axis=-1, keepdims=True)
        y = x * lax.rsqrt(ms + EPS) * g_ref[...]
        h = y * (1.0 + mod_ref[0, 1:2, :]) + mod_ref[0, 0:1, :]
        h_scr[...] = h.astype(BF16)
        pos = pos_ref[...].astype(F32)
        lane = lax.broadcasted_iota(I32, (1, LANES), 1)
        a128 = pos * invf_ref[0:1, :]
        c128, s128 = jnp.cos(a128), jnp.sin(a128)
        tab_scr[0] = jnp.where(lane < 32, c128, 1.0)
        tab_scr[1] = jnp.where(lane < 16, -s128, 0.0)
        tab_scr[2] = jnp.where((lane >= 16) & (lane < 32), s128, 0.0)
        l64 = lane & 63
        a64 = pos * invf_ref[1:2, :]
        c64, s64 = jnp.cos(a64), jnp.sin(a64)
        tab_scr[3] = jnp.where(l64 < 16, c64, 1.0)
        tab_scr[4] = jnp.where(l64 < 8, -s64, 0.0)
        tab_scr[5] = jnp.where((l64 >= 8) & (l64 < 16), s64, 0.0)

    acc = jnp.dot(h_scr[...], w_ref[...], preferred_element_type=F32)

    def sub(s):
        return acc[:, s * LANES:(s + 1) * LANES]

    def rope128(xb):
        return _rope(xb, tab_scr[0], tab_scr[1], tab_scr[2], 16)

    def rope64(xb):
        return _rope(xb, tab_scr[3], tab_scr[4], tab_scr[5], 8)

    def store(s, val):
        o_ref[:, s * LANES:(s + 1) * LANES] = val.astype(BF16)

    @pl.when(j < 6)
    def _():
        o_ref[...] = acc.astype(BF16)

    @pl.when((j == 6) | (j == 7))
    def _():
        for s in range(nsub):
            store(s, rope128(sub(s)))

    @pl.when((j == 8) | (j == 9))
    def _():
        for s in range(nsub):
            store(s, rope64(sub(s)))

    @pl.when(j == 10)
    def _():
        store(0, rope128(sub(0)))
        v = sub(1)
        store(1, v)
        vt_ref[0] = v.T.astype(BF16)
        store(2, rope64(sub(2)))
        w = sub(3)
        store(3, w)
        wt_ref[0] = (w * idx_scale).T[0:IDX_HEADS, :]


def _proj(x2d, mod3, g1, pos_col, invf, w_in_p, batch, seq, idx_scale, tm=1024, tn=512):
    rows, d = x2d.shape
    ncols = w_in_p.shape[1]
    tiles_per_batch = seq // tm
    kern = functools.partial(_proj_kernel, idx_scale=idx_scale)
    return pl.pallas_call(
        kern,
        out_shape=(
            jax.ShapeDtypeStruct((rows, ncols), BF16),
            jax.ShapeDtypeStruct((batch, HEAD_DIM, seq), BF16),
            jax.ShapeDtypeStruct((batch, IDX_HEADS, seq), F32),
        ),
        grid=(rows // tm, ncols // tn),
        in_specs=[
            pl.BlockSpec((tm, d), lambda i, j: (i, 0)),
            pl.BlockSpec((1, 6, d), lambda i, j: (i // tiles_per_batch, 0, 0)),
            pl.BlockSpec((1, d), lambda i, j: (0, 0)),
            pl.BlockSpec((tm, 1), lambda i, j: (i, 0)),
            pl.BlockSpec((8, LANES), lambda i, j: (0, 0)),
            pl.BlockSpec((d, tn), lambda i, j: (0, j)),
        ],
        out_specs=(
            pl.BlockSpec((tm, tn), lambda i, j: (i, j)),
            pl.BlockSpec((1, HEAD_DIM, tm),
                         lambda i, j: (i // tiles_per_batch, 0, i % tiles_per_batch)),
            pl.BlockSpec((1, IDX_HEADS, tm),
                         lambda i, j: (i // tiles_per_batch, 0, i % tiles_per_batch)),
        ),
        scratch_shapes=[
            pltpu.VMEM((tm, d), BF16),
            pltpu.VMEM((6, tm, LANES), F32),
        ],
        compiler_params=_cparams(("arbitrary", "arbitrary")),
        name="proj",
    )(x2d, mod3, g1, pos_col, invf, w_in_p)


SUB = 128


def _sb_kernel(q_ref, k_ref, v_ref, tri_ref, o_ref, carry_scr, acc_scr, *, scale, kc):
    i = pl.program_id(2)
    tq = q_ref.shape[1]
    nsub = kc // SUB
    n_diag = tq // kc
    q = q_ref[0]
    tri = tri_ref[...]
    t_row = i * tq + lax.broadcasted_iota(I32, (tq, kc), 0)

    def chunk(j, diag):
        start = pl.multiple_of(j * kc, kc)
        k = k_ref[0, pl.ds(start, kc), :]
        v = v_ref[0, pl.ds(start, kc), :]
        zn = lax.dot_general(q, k, NT_DIMS, preferred_element_type=F32) * (-scale)
        sp = jnp.log(1.0 + jnp.exp2(jnp.abs(zn) * (-LOG2E)))
        lneg = jnp.minimum(zn, 0.0) - sp
        lpos = lneg - zn
        if diag:
            strict = (start + lax.broadcasted_iota(I32, (tq, kc), 1)) < t_row
            lneg = jnp.where(strict, lneg, 0.0)
        hi = lneg.astype(BF16)
        lo = (lneg - hi.astype(F32)).astype(BF16)
        carry = carry_scr[...]
        parts = [None] * nsub
        for s in reversed(range(nsub)):
            cols = slice(s * SUB, (s + 1) * SUB)
            r = jnp.dot(jnp.concatenate([hi[:, cols], lo[:, cols]], axis=1), tri,
                        preferred_element_type=F32)
            parts[s] = lpos[:, cols] + r[:, :SUB] + carry
            carry = carry + r[:, SUB:]
        carry_scr[...] = carry
        w = jnp.exp(jnp.concatenate(parts, axis=1))
        if diag:
            w = jnp.where(strict, w, 0.0)
        acc_scr[...] += jnp.dot(w.astype(BF16), v, preferred_element_type=F32)

    carry_scr[...] = jnp.zeros(carry_scr.shape, F32)
    acc_scr[...] = jnp.zeros(acc_scr.shape, F32)
    first = i * n_diag
    for n in range(n_diag):
        chunk(first + n_diag - 1 - n, True)

    def body(n, _):
        for u in range(n_diag):
            chunk(first - 1 - n_diag * n - u, False)
        return 0

    lax.fori_loop(0, i, body, 0)
    o_ref[0] = acc_scr[...]


def _sb(proj3, tri, batch, seq, n_heads, scale, tq=512, kc=256):
    kern = functools.partial(_sb_kernel, scale=scale, kc=kc)
    return pl.pallas_call(
        kern,
        out_shape=jax.ShapeDtypeStruct((batch, seq, n_heads * HEAD_DIM), F32),
        grid=(batch, n_heads, seq // tq),
        in_specs=[
            pl.BlockSpec((1, tq, HEAD_DIM), lambda b, h, i: (b, i, h)),
            pl.BlockSpec((1, seq, HEAD_DIM), lambda b, h, i: (b, 0, n_heads + h)),
            pl.BlockSpec((1, seq, HEAD_DIM), lambda b, h, i: (b, 0, 2 * n_heads + h)),
            pl.BlockSpec((2 * SUB, 2 * SUB), lambda b, h, i: (0, 0)),
        ],
        out_specs=pl.BlockSpec((1, tq, HEAD_DIM), lambda b, h, i: (b, i, h)),
        scratch_shapes=[pltpu.VMEM((tq, SUB), F32), pltpu.VMEM((tq, HEAD_DIM), F32)],
        compiler_params=_cparams(("arbitrary", "arbitrary", "arbitrary")),
        name="sb",
    )(proj3, proj3, proj3, tri)


def _dsa_kernel(qd_ref, qi_ref, kd_ref, vt_ref, ki_ref, wt_ref, o_ref,
                qm_scr, qa_scr, key_scr, acc_scr, m_scr, l_scr, tau_scr, s_scr, p_scr, alpha_scr,
                *, scale, topk, kc, idx_bits):
    i = pl.program_id(1)
    tq = qd_ref.shape[1]
    n_dsa = qd_ref.shape[2] // HEAD_DIM
    n_chunks = ((i + 1) * tq + kc - 1) // kc
    ks = 2 * kc
    n_trips = (n_chunks + 1) // 2
    t_row = i * tq + lax.broadcasted_iota(I32, (1, tq), 1)

    lane = lax.broadcasted_iota(I32, (tq, LANES), 1)
    for h in range(IDX_HEADS):
        blk = qi_ref[0, :, (h // 2) * LANES:(h // 2 + 1) * LANES]
        keep = (lane < IDX_DIM) if h % 2 == 0 else (lane >= IDX_DIM)
        qm_scr[h * tq:(h + 1) * tq, :] = jnp.where(keep, blk, jnp.zeros_like(blk))
    for h in range(n_dsa):
        qa_scr[h * tq:(h + 1) * tq, :] = qd_ref[0, :, h * HEAD_DIM:(h + 1) * HEAD_DIM]

    def score_chunk(c, masked):
        start = pl.multiple_of(c * kc, kc)
        kch = ki_ref[0, pl.ds(start, kc), :]
        r = lax.dot_general(kch, qm_scr[...], NT_DIMS, preferred_element_type=F32)
        sc = jnp.zeros((kc, tq), F32)
        for h in range(IDX_HEADS):
            sc = sc + wt_ref[0, h:h + 1, :] * jnp.maximum(r[:, h * tq:(h + 1) * tq], 0.0)
        if masked:
            kidx = start + lax.broadcasted_iota(I32, (kc, tq), 0)
            sc = jnp.where(kidx <= t_row, sc, -jnp.inf)
        key_scr[pl.ds(start, kc), :] = sc

    def score_body(c, _):
        score_chunk(c, False)
        return 0

    lax.fori_loop(0, n_chunks - 1, score_body, 0)
    score_chunk(n_chunks - 1, True)
    key_scr[pl.ds(pl.multiple_of(n_chunks * kc, kc), kc), :] = jnp.full((kc, tq), -jnp.inf, F32)

    def as_float(ordered):
        return pltpu.bitcast(jnp.where(ordered >= 0, ordered, ordered ^ 0x7FFFFFFF), F32)

    def count(pred):
        def body(t, cnt):
            start = pl.multiple_of(t * ks, ks)
            hit = jnp.where(pred(key_scr[pl.ds(start, ks), :], start), 1, 0)
            parts = [hit[r * 8:(r + 1) * 8, :] for r in range(ks // 8)]
            while len(parts) > 1:
                parts = [a + b for a, b in zip(parts[0::2], parts[1::2])]
            return cnt + parts[0]

        cnt8 = lax.fori_loop(0, n_trips, body, jnp.zeros((8, tq), I32))
        return jnp.sum(cnt8, axis=0, keepdims=True)

    def tau_body(b, ordered):
        cand = ordered + lax.shift_left(jnp.int32(1), 31 - b)
        cand_f = as_float(cand)
        return jnp.where(count(lambda sc, start: sc >= cand_f) >= topk, cand, ordered)

    ordered = lax.fori_loop(0, 32, tau_body, jnp.full((1, tq), INT_MIN, I32))
    tau = jnp.where(ordered == INT_MIN, jnp.finfo(F32).min, as_float(ordered))
    tau_scr[...] = tau
    cnt_ge = count(lambda sc, start: sc >= tau)

    @pl.when(jnp.max(cnt_ge) > topk)
    def _():
        cnt_gt = count(lambda sc, start: sc > tau)

        def cut_body(b, cut):
            cand = cut + lax.shift_left(jnp.int32(1), idx_bits - 1 - b)

            def pred(sc, start):
                kidx = start + lax.broadcasted_iota(I32, (ks, tq), 0)
                return (sc == tau) & (kidx < cand)

            return jnp.where(cnt_gt + count(pred) < topk, cand, cut)

        cut = lax.fori_loop(0, idx_bits, cut_body, jnp.zeros((1, tq), I32))

        def drop_body(t, _):
            start = pl.multiple_of(t * ks, ks)
            sc = key_scr[pl.ds(start, ks), :]
            kidx = start + lax.broadcasted_iota(I32, (ks, tq), 0)
            key_scr[pl.ds(start, ks), :] = jnp.where((sc == tau) & (kidx > cut), -jnp.inf, sc)
            return 0

        lax.fori_loop(0, n_trips, drop_body, 0)

    m_scr[...] = jnp.full(m_scr.shape, NEG_BIG, F32)
    l_scr[...] = jnp.zeros(l_scr.shape, F32)
    acc_scr[...] = jnp.zeros(acc_scr.shape, F32)
    c2 = scale * LOG2E

    def logits(c):
        kd = kd_ref[0, pl.ds(pl.multiple_of(c * kc, kc), kc), :]
        return lax.dot_general(kd, qa_scr[...], NT_DIMS, preferred_element_type=F32)

    def weighted_values(c):
        vt = vt_ref[0, :, pl.ds(pl.multiple_of(c * kc, kc), kc)]
        acc_scr[...] = alpha_scr[...] * acc_scr[...] + jnp.dot(
            vt, p_scr[...], preferred_element_type=F32)

    s_scr[...] = logits(0)
    p_scr[...] = jnp.zeros(p_scr.shape, BF16)
    alpha_scr[...] = jnp.ones(alpha_scr.shape, F32)

    def att_body(c, _):
        weighted_values(jnp.maximum(c - 1, 0))
        nxt = logits(jnp.minimum(c + 1, n_chunks - 1))
        sel = key_scr[pl.ds(pl.multiple_of(c * kc, kc), kc), :] >= tau_scr[...]
        s = s_scr[...]
        s = jnp.concatenate([jnp.where(sel, s[:, h * tq:(h + 1) * tq], NEG_BIG)
                             for h in range(n_dsa)], axis=1)
        m_old = m_scr[...]
        m_new = jnp.maximum(m_old, jnp.max(s, axis=0, keepdims=True))
        alpha = jnp.exp2((m_old - m_new) * c2)
        p = jnp.exp2((s - m_new) * c2)
        l_scr[...] = alpha * l_scr[...] + jnp.sum(p, axis=0, keepdims=True)
        m_scr[...] = m_new
        alpha_scr[...] = alpha
        p_scr[...] = p.astype(BF16)
        s_scr[...] = nxt
        return 0

    lax.fori_loop(0, n_chunks, att_body, 0)
    weighted_values(n_chunks - 1)

    out_t = acc_scr[...] / l_scr[...]
    for h in range(n_dsa):
        o_ref[0, :, h * HEAD_DIM:(h + 1) * HEAD_DIM] = out_t[:, h * tq:(h + 1) * tq].T


def _dsa(proj3, v_t, w_t, batch, seq, n_dsa, scale, tq=128, kc=256):
    width = n_dsa * HEAD_DIM
    topk = min(TOPK_MAX, seq // 4)
    idx_bits = int(np.ceil(np.log2(seq)))
    kern = functools.partial(_dsa_kernel, scale=scale, topk=topk, kc=kc, idx_bits=idx_bits)
    assert width == 1024 and IDX_HEADS * IDX_DIM == 1024
    misc = 5 * 1024 // LANES
    return pl.pallas_call(
        kern,
        out_shape=jax.ShapeDtypeStruct((batch, seq, width), F32),
        grid=(batch, seq // tq),
        in_specs=[
            pl.BlockSpec((1, tq, width), lambda b, i: (b, i, 3)),
            pl.BlockSpec((1, tq, IDX_HEADS * IDX_DIM), lambda b, i: (b, i, 4)),
            pl.BlockSpec((1, seq, HEAD_DIM), lambda b, i: (b, 0, misc)),
            pl.BlockSpec((1, HEAD_DIM, seq), lambda b, i: (b, 0, 0)),
            pl.BlockSpec((1, seq, LANES), lambda b, i: (b, 0, misc + 2)),
            pl.BlockSpec((1, IDX_HEADS, tq), lambda b, i: (b, 0, i)),
        ],
        out_specs=pl.BlockSpec((1, tq, width), lambda b, i: (b, i, 0)),
        scratch_shapes=[
            pltpu.VMEM((IDX_HEADS * tq, LANES), BF16),
            pltpu.VMEM((n_dsa * tq, HEAD_DIM), BF16),
            pltpu.VMEM((seq + kc, tq), F32),
            pltpu.VMEM((HEAD_DIM, n_dsa * tq), F32),
            pltpu.VMEM((1, n_dsa * tq), F32),
            pltpu.VMEM((1, n_dsa * tq), F32),
            pltpu.VMEM((1, tq), F32),
            pltpu.VMEM((kc, n_dsa * tq), F32),
            pltpu.VMEM((kc, n_dsa * tq), BF16),
            pltpu.VMEM((1, n_dsa * tq), F32),
        ],
        compiler_params=_cparams(("arbitrary", "arbitrary")),
        name="dsa",
    )(proj3, proj3, proj3, v_t, proj3, w_t)


def _rms(v, g):
    return v * lax.rsqrt(jnp.mean(v * v, axis=-1, keepdims=True) + EPS) * g


def _outp_kernel(osb_ref, ods_ref, x_ref, mod_ref, gsb_ref, gds_ref, g2_ref, w_ref,
                 x1_ref, h2_ref):
    half = osb_ref.shape[1]
    a = _rms(osb_ref[...], gsb_ref[...]).astype(BF16)
    b = _rms(ods_ref[...], gds_ref[...]).astype(BF16)
    y = (jnp.dot(a, w_ref[0:half, :], preferred_element_type=F32)
         + jnp.dot(b, w_ref[half:, :], preferred_element_type=F32))
    x1 = x_ref[...] + mod_ref[0, 2:3, :] * y
    x1_ref[...] = x1
    h2 = _rms(x1, g2_ref[...]) * (1.0 + mod_ref[0, 4:5, :]) + mod_ref[0, 3:4, :]
    h2_ref[...] = h2.astype(BF16)


def _outp(o_sb, o_ds, x2d, mod3, g_sb, g_ds, g2, w_out, seq, tm=512):
    rows, d = x2d.shape
    half = o_sb.shape[1]
    tiles_per_batch = seq // tm
    return pl.pallas_call(
        _outp_kernel,
        out_shape=(jax.ShapeDtypeStruct((rows, d), F32), jax.ShapeDtypeStruct((rows, d), BF16)),
        grid=(rows // tm,),
        in_specs=[
            pl.BlockSpec((tm, half), lambda i: (i, 0)),
            pl.BlockSpec((tm, half), lambda i: (i, 0)),
            pl.BlockSpec((tm, d), lambda i: (i, 0)),
            pl.BlockSpec((1, 6, d), lambda i: (i // tiles_per_batch, 0, 0)),
            pl.BlockSpec((1, half), lambda i: (0, 0)),
            pl.BlockSpec((1, half), lambda i: (0, 0)),
            pl.BlockSpec((1, d), lambda i: (0, 0)),
            pl.BlockSpec((2 * half, d), lambda i: (0, 0)),
        ],
        out_specs=(pl.BlockSpec((tm, d), lambda i: (i, 0)), pl.BlockSpec((tm, d), lambda i: (i, 0))),
        compiler_params=_cparams(("arbitrary",)),
        name="outp",
    )(o_sb, o_ds, x2d, mod3, g_sb, g_ds, g2, w_out)


def _ffn_kernel(h_ref, halo_ref, x1_ref, mod_ref, wg_ref, wv_ref, cwg_ref, cwv_ref,
                cbg_ref, cbv_ref, wd_ref, gf_ref, o_ref, *, tiles_per_batch, final_norm):
    i = pl.program_id(0)
    j = pl.program_id(1)
    tm = h_ref.shape[0]
    pad = halo_ref.shape[0]
    h = h_ref[...]
    halo = jnp.where(i % tiles_per_batch == 0, jnp.zeros_like(halo_ref[...]), halo_ref[...])

    row8 = lax.broadcasted_iota(I32, (8, 1), 0)

    def conv(w_ref, cw_ref, cb_ref, cols):
        w = w_ref[:, cols]
        u = jnp.dot(h, w, preferred_element_type=F32)
        tail = jnp.dot(halo, w, preferred_element_type=F32)[pad - 8:, :]
        uc = cb_ref[:, cols] + u * cw_ref[CONV_WIDTH - 1:CONV_WIDTH, cols]
        for tap in range(CONV_WIDTH - 1):
            back = CONV_WIDTH - 1 - tap
            rolled = pltpu.roll(u, back, axis=0)
            head = jnp.where(row8 < back, pltpu.roll(tail, back, axis=0), rolled[:8, :])
            shifted = jnp.concatenate([head, rolled[8:, :]], axis=0)
            uc = uc + shifted * cw_ref[tap:tap + 1, cols]
        return uc

    tn = wg_ref.shape[1]
    part = jnp.where(j > 0, o_ref[...], 0.0)
    for cols in (slice(0, tn // 2), slice(tn // 2, tn)):
        gate = conv(wg_ref, cwg_ref, cbg_ref, cols)
        val = conv(wv_ref, cwv_ref, cbv_ref, cols)
        a = (gate * jax.nn.sigmoid(gate) * val).astype(BF16)
        part = part + jnp.dot(a, wd_ref[cols, :], preferred_element_type=F32)
    o_ref[...] = part

    @pl.when(j == pl.num_programs(1) - 1)
    def _():
        x2 = x1_ref[...] + mod_ref[0, 5:6, :] * o_ref[...]
        o_ref[...] = _rms(x2, gf_ref[...]) if final_norm else x2


def _ffn(h2, x1, mod3, w_up_g, w_up_v, cw_g, cw_v, cb_g, cb_v, w_dn, g_f, seq, final_norm,
         tm=1024, tn=512):
    rows, d = x1.shape
    dff = w_up_g.shape[1]
    tiles_per_batch = seq // tm
    pad = 16
    kern = functools.partial(_ffn_kernel, tiles_per_batch=tiles_per_batch, final_norm=final_norm)
    return pl.pallas_call(
        kern,
        out_shape=jax.ShapeDtypeStruct((rows, d), F32),
        grid=(rows // tm, dff // tn),
        in_specs=[
            pl.BlockSpec((tm, d), lambda i, j: (i, 0), pipeline_mode=pl.Buffered(1)),
            pl.BlockSpec((pad, d), lambda i, j: (jnp.maximum(i * (tm // pad) - 1, 0), 0)),
            pl.BlockSpec((tm, d), lambda i, j: (i, 0), pipeline_mode=pl.Buffered(1)),
            pl.BlockSpec((1, 6, d), lambda i, j: (i // tiles_per_batch, 0, 0)),
            pl.BlockSpec((d, tn), lambda i, j: (0, j)),
            pl.BlockSpec((d, tn), lambda i, j: (0, j)),
            pl.BlockSpec((CONV_WIDTH, tn), lambda i, j: (0, j)),
            pl.BlockSpec((CONV_WIDTH, tn), lambda i, j: (0, j)),
            pl.BlockSpec((1, tn), lambda i, j: (0, j)),
            pl.BlockSpec((1, tn), lambda i, j: (0, j)),
            pl.BlockSpec((tn, d), lambda i, j: (j, 0)),
            pl.BlockSpec((1, d), lambda i, j: (0, 0)),
        ],
        out_specs=pl.BlockSpec((tm, d), lambda i, j: (i, 0), pipeline_mode=pl.Buffered(1)),
        compiler_params=_cparams(("arbitrary", "arbitrary")),
        name="ffn",
    )(h2, h2, x1, mod3, w_up_g, w_up_v, cw_g, cw_v, cb_g, cb_v, w_dn, g_f)


def _pad_cols(a, n):
    return jnp.pad(a, ((0, 0), (0, n - a.shape[1])))


def _regroup_w_in(w, d_model):
    sbw = d_model // 2
    p = np.cumsum([0, sbw, sbw, sbw, sbw, HEAD_DIM, HEAD_DIM, IDX_HEADS * IDX_DIM, IDX_DIM, IDX_HEADS])
    q_sb, k_sb, v_sb, q_ds, k_ds, v_ds, q_ix, k_ix, w_ix = [w[:, p[n]:p[n + 1]] for n in range(9)]
    return jnp.concatenate(
        [q_sb, k_sb, v_sb, q_ds, q_ix, k_ds, v_ds, k_ix, k_ix, _pad_cols(w_ix, LANES)], axis=1)


def kernel(x, c, positions, w_ada, b_ada, norm1_g, w_in, sb_norm_g, dsa_norm_g, w_out, norm2_g,
           w_up, conv_w, conv_b, w_down, final_norm_g):
    batch, seq, d = x.shape
    depth = w_ada.shape[0]
    n_sb = (d // HEAD_DIM) // 2
    n_dsa = d // HEAD_DIM - n_sb
    d_ff = w_down.shape[1]
    ff_pad = -(-d_ff // 512) * 512
    idx_scale = (IDX_HEADS ** -0.5) * (IDX_DIM ** -0.5)
    scale = HEAD_DIM ** -0.5
    rows = batch * seq

    half128 = HEAD_DIM // ROPE_FRACTION // 2
    half64 = IDX_DIM // ROPE_FRACTION // 2
    f128 = ROPE_THETA ** (-jnp.arange(half128, dtype=F32) / half128)
    f64 = ROPE_THETA ** (-jnp.arange(half64, dtype=F32) / half64)
    invf = jnp.zeros((8, LANES), F32)
    invf = invf.at[0].set(jnp.tile(f128, LANES // half128)).at[1].set(jnp.tile(f64, LANES // half64))

    tri_np = np.zeros((2 * SUB, 2 * SUB), np.float32)
    later = (np.arange(SUB)[:, None] > np.arange(SUB)[None, :]).astype(np.float32)
    tri_np[:SUB, :SUB] = later
    tri_np[SUB:, :SUB] = later
    tri_np[:, SUB:] = 1.0
    tri = jnp.asarray(tri_np, BF16)

    pos_col = positions.reshape(rows, 1)
    c_pad = jnp.pad(c, ((0, 16 - batch), (0, 0)))
    x2d = x.reshape(rows, d)

    for l in range(depth):
        mod = _ada(c_pad, w_ada[l], b_ada[l][None, :])[:batch]
        mod3 = mod.reshape(batch, 6, d)

        w_in_p = _regroup_w_in(w_in[l].astype(BF16), d)
        proj, v_t, w_t = _proj(x2d, mod3, norm1_g[l][None, :], pos_col, invf, w_in_p,
                               batch, seq, idx_scale)
        proj3 = proj.reshape(batch, seq, proj.shape[1])

        o_sb = _sb(proj3, tri, batch, seq, n_sb, scale)
        o_ds = _dsa(proj3, v_t, w_t, batch, seq, n_dsa, scale)

        x1, h2 = _outp(o_sb.reshape(rows, -1), o_ds.reshape(rows, -1), x2d, mod3,
                       sb_norm_g[l][None, :], dsa_norm_g[l][None, :], norm2_g[l][None, :],
                       w_out[l].astype(BF16), seq)

        wu = w_up[l].astype(BF16)
        w_up_g = _pad_cols(wu[:, :d_ff], ff_pad)
        w_up_v = _pad_cols(wu[:, d_ff:], ff_pad)
        cw_g = _pad_cols(conv_w[l][:, :d_ff], ff_pad)
        cw_v = _pad_cols(conv_w[l][:, d_ff:], ff_pad)
        cb_g = _pad_cols(conv_b[l][None, :d_ff], ff_pad)
        cb_v = _pad_cols(conv_b[l][None, d_ff:], ff_pad)
        w_dn = jnp.pad(w_down[l].astype(BF16), ((0, ff_pad - d_ff), (0, 0)))
        x2d = _ffn(h2, x1, mod3, w_up_g, w_up_v, cw_g, cw_v, cb_g, cb_v, w_dn,
                   final_norm_g[None, :], seq, final_norm=(l == depth - 1))
    return x2d.reshape(batch, seq, d)
```

```python
import functools

import jax
import jax.numpy as jnp
import numpy as np
from jax import lax
from jax.experimental import pallas as pl
from jax.experimental.pallas import tpu as pltpu

F32 = jnp.float32
BF16 = jnp.bfloat16
I32 = jnp.int32

HEAD_DIM = 128
IDX_HEADS = 16
IDX_DIM = 64
TOPK_MAX = 256
ROPE_THETA = 500000.0
ROPE_FRACTION = 4
CONV_WIDTH = 3
EPS = 1e-6
NEG_BIG = -1e30
INT_MIN = -(2 ** 31)
LOG2E = 1.4426950408889634

LANES = 128
VMEM_LIMIT = 56 * 1024 * 1024

NT_DIMS = (((1,), (1,)), ((), ()))


def _cparams(sem):
    return pltpu.CompilerParams(dimension_semantics=sem, vmem_limit_bytes=VMEM_LIMIT)


def _ada_kernel(c_ref, w_ref, b_ref, o_ref):
    c = c_ref[...]
    s = (c * jax.nn.sigmoid(c)).astype(BF16)
    o_ref[...] = jnp.dot(s, w_ref[...].astype(BF16), preferred_element_type=F32) + b_ref[...]


def _ada(c_pad, w_ada, b_ada, tn=1024):
    m, d = c_pad.shape
    n = w_ada.shape[1]
    return pl.pallas_call(
        _ada_kernel,
        out_shape=jax.ShapeDtypeStruct((m, n), F32),
        grid=(n // tn,),
        in_specs=[
            pl.BlockSpec((m, d), lambda j: (0, 0)),
            pl.BlockSpec((d, tn), lambda j: (0, j)),
            pl.BlockSpec((1, tn), lambda j: (0, j)),
        ],
        out_specs=pl.BlockSpec((m, tn), lambda j: (0, j)),
        compiler_params=_cparams(("arbitrary",)),
        name="ada",
    )(c_pad, w_ada, b_ada)


def _rope(xb, c, sn, sp, shift):
    return (xb * c + pltpu.roll(xb, LANES - shift, axis=1) * sn
            + pltpu.roll(xb, shift, axis=1) * sp)


def _proj_kernel(x_ref, mod_ref, g_ref, pos_ref, invf_ref, w_ref,
                 o_ref, vt_ref, wt_ref, h_scr, tab_scr, *, idx_scale):
    j = pl.program_id(1)
    tn = o_ref.shape[1]
    nsub = tn // LANES

    @pl.when(j == 0)
    def _():
        x = x_ref[...]
        ms = jnp.mean(x * x, axis=-1, keepdims=True)
        y = x * lax.rsqrt(ms + EPS) * g_ref[...]
        h = y * (1.0 + mod_ref[0, 1:2, :]) + mod_ref[0, 0:1, :]
        h_scr[...] = h.astype(BF16)
        pos = pos_ref[...].astype(F32)
        lane = lax.broadcasted_iota(I32, (1, LANES), 1)
        a128 = pos * invf_ref[0:1, :]
        c128, s128 = jnp.cos(a128), jnp.sin(a128)
        tab_scr[0] = jnp.where(lane < 32, c128, 1.0)
        tab_scr[1] = jnp.where(lane < 16, -s128, 0.0)
        tab_scr[2] = jnp.where((lane >= 16) & (lane < 32), s128, 0.0)
        l64 = lane & 63
        a64 = pos * invf_ref[1:2, :]
        c64, s64 = jnp.cos(a64), jnp.sin(a64)
        tab_scr[3] = jnp.where(l64 < 16, c64, 1.0)
        tab_scr[4] = jnp.where(l64 < 8, -s64, 0.0)
        tab_scr[5] = jnp.where((l64 >= 8) & (l64 < 16), s64, 0.0)

    acc = jnp.dot(h_scr[...], w_ref[...], preferred_element_type=F32)

    def sub(s):
        return acc[:, s * LANES:(s + 1) * LANES]

    def rope128(xb):
        return _rope(xb, tab_scr[0], tab_scr[1], tab_scr[2], 16)

    def rope64(xb):
        return _rope(xb, tab_scr[3], tab_scr[4], tab_scr[5], 8)

    def store(s, val):
        o_ref[:, s * LANES:(s + 1) * LANES] = val.astype(BF16)

    @pl.when(j < 6)
    def _():
        o_ref[...] = acc.astype(BF16)

    @pl.when((j == 6) | (j == 7))
    def _():
        for s in range(nsub):
            store(s, rope128(sub(s)))

    @pl.when((j == 8) | (j == 9))
    def _():
        for s in range(nsub):
            store(s, rope64(sub(s)))

    @pl.when(j == 10)
    def _():
        store(0, rope128(sub(0)))
        v = sub(1)
        store(1, v)
        vt_ref[0] = v.T.astype(BF16)
        store(2, rope64(sub(2)))
        w = sub(3)
        store(3, w)
        wt_ref[0] = (w * idx_scale).T[0:IDX_HEADS, :]


def _proj(x2d, mod3, g1, pos_col, invf, w_in_p, batch, seq, idx_scale, tm=1024, tn=512):
    rows, d = x2d.shape
    ncols = w_in_p.shape[1]
    tiles_per_batch = seq // tm
    kern = functools.partial(_proj_kernel, idx_scale=idx_scale)
    return pl.pallas_call(
        kern,
        out_shape=(
            jax.ShapeDtypeStruct((rows, ncols), BF16),
            jax.ShapeDtypeStruct((batch, HEAD_DIM, seq), BF16),
            jax.ShapeDtypeStruct((batch, IDX_HEADS, seq), F32),
        ),
        grid=(rows // tm, ncols // tn),
        in_specs=[
            pl.BlockSpec((tm, d), lambda i, j: (i, 0)),
            pl.BlockSpec((1, 6, d), lambda i, j: (i // tiles_per_batch, 0, 0)),
            pl.BlockSpec((1, d), lambda i, j: (0, 0)),
            pl.BlockSpec((tm, 1), lambda i, j: (i, 0)),
            pl.BlockSpec((8, LANES), lambda i, j: (0, 0)),
            pl.BlockSpec((d, tn), lambda i, j: (0, j)),
        ],
        out_specs=(
            pl.BlockSpec((tm, tn), lambda i, j: (i, j)),
            pl.BlockSpec((1, HEAD_DIM, tm),
                         lambda i, j: (i // tiles_per_batch, 0, i % tiles_per_batch)),
            pl.BlockSpec((1, IDX_HEADS, tm),
                         lambda i, j: (i // tiles_per_batch, 0, i % tiles_per_batch)),
        ),
        scratch_shapes=[
            pltpu.VMEM((tm, d), BF16),
            pltpu.VMEM((6, tm, LANES), F32),
        ],
        compiler_params=_cparams(("arbitrary", "arbitrary")),
        name="proj",
    )(x2d, mod3, g1, pos_col, invf, w_in_p)


SUB = 128


def _sb_kernel(q_ref, k_ref, v_ref, tri_ref, o_ref, carry_scr, acc_scr, *, scale, kc):
    i = pl.program_id(2)
    tq = q_ref.shape[1]
    nsub = kc // SUB
    n_diag = tq // kc
    q = q_ref[0]
    tri = tri_ref[...]
    t_row = i * tq + lax.broadcasted_iota(I32, (tq, kc), 0)

    def chunk(j, diag):
        start = pl.multiple_of(j * kc, kc)
        k = k_ref[0, pl.ds(start, kc), :]
        v = v_ref[0, pl.ds(start, kc), :]
        zn = lax.dot_general(q, k, NT_DIMS, preferred_element_type=F32) * (-scale)
        sp = jnp.log(1.0 + jnp.exp2(jnp.abs(zn) * (-LOG2E)))
        lneg = jnp.minimum(zn, 0.0) - sp
        lpos = lneg - zn
        if diag:
            strict = (start + lax.broadcasted_iota(I32, (tq, kc), 1)) < t_row
            lneg = jnp.where(strict, lneg, 0.0)
        hi = lneg.astype(BF16)
        lo = (lneg - hi.astype(F32)).astype(BF16)
        carry = carry_scr[...]
        parts = [None] * nsub
        for s in reversed(range(nsub)):
            cols = slice(s * SUB, (s + 1) * SUB)
            r = jnp.dot(jnp.concatenate([hi[:, cols], lo[:, cols]], axis=1), tri,
                        preferred_element_type=F32)
            parts[s] = lpos[:, cols] + r[:, :SUB] + carry
            carry = carry + r[:, SUB:]
        carry_scr[...] = carry
        w = jnp.exp(jnp.concatenate(parts, axis=1))
        if diag:
            w = jnp.where(strict, w, 0.0)
        acc_scr[...] += jnp.dot(w.astype(BF16), v, preferred_element_type=F32)

    carry_scr[...] = jnp.zeros(carry_scr.shape, F32)
    acc_scr[...] = jnp.zeros(acc_scr.shape, F32)
    first = i * n_diag
    for n in range(n_diag):
        chunk(first + n_diag - 1 - n, True)

    def body(n, _):
        for u in range(n_diag):
            chunk(first - 1 - n_diag * n - u, False)
        return 0

    lax.fori_loop(0, i, body, 0)
    o_ref[0] = acc_scr[...]


def _sb(proj3, tri, batch, seq, n_heads, scale, tq=512, kc=256):
    kern = functools.partial(_sb_kernel, scale=scale, kc=kc)
    return pl.pallas_call(
        kern,
        out_shape=jax.ShapeDtypeStruct((batch, seq, n_heads * HEAD_DIM), F32),
        grid=(batch, n_heads, seq // tq),
        in_specs=[
            pl.BlockSpec((1, tq, HEAD_DIM), lambda b, h, i: (b, i, h)),
            pl.BlockSpec((1, seq, HEAD_DIM), lambda b, h, i: (b, 0, n_heads + h)),
            pl.BlockSpec((1, seq, HEAD_DIM), lambda b, h, i: (b, 0, 2 * n_heads + h)),
            pl.BlockSpec((2 * SUB, 2 * SUB), lambda b, h, i: (0, 0)),
        ],
        out_specs=pl.BlockSpec((1, tq, HEAD_DIM), lambda b, h, i: (b, i, h)),
        scratch_shapes=[pltpu.VMEM((tq, SUB), F32), pltpu.VMEM((tq, HEAD_DIM), F32)],
        compiler_params=_cparams(("arbitrary", "arbitrary", "arbitrary")),
        name="sb",
    )(proj3, proj3, proj3, tri)


def _dsa_kernel(qd_ref, qi_ref, kd_ref, vt_ref, ki_ref, wt_ref, o_ref,
                qm_scr, qa_scr, key_scr, acc_scr, m_scr, l_scr, tau_scr, s_scr, p_scr, alpha_scr,
                *, scale, topk, kc, idx_bits):
    i = pl.program_id(1)
    tq = qd_ref.shape[1]
    n_dsa = qd_ref.shape[2] // HEAD_DIM
    n_chunks = ((i + 1) * tq + kc - 1) // kc
    ks = 2 * kc
    n_trips = (n_chunks + 1) // 2
    t_row = i * tq + lax.broadcasted_iota(I32, (1, tq), 1)

    lane = lax.broadcasted_iota(I32, (tq, LANES), 1)
    for h in range(IDX_HEADS):
        blk = qi_ref[0, :, (h // 2) * LANES:(h // 2 + 1) * LANES]
        keep = (lane < IDX_DIM) if h % 2 == 0 else (lane >= IDX_DIM)
        qm_scr[h * tq:(h + 1) * tq, :] = jnp.where(keep, blk, jnp.zeros_like(blk))
    for h in range(n_dsa):
        qa_scr[h * tq:(h + 1) * tq, :] = qd_ref[0, :, h * HEAD_DIM:(h + 1) * HEAD_DIM]

    def score_chunk(c, masked):
        start = pl.multiple_of(c * kc, kc)
        kch = ki_ref[0, pl.ds(start, kc), :]
        r = lax.dot_general(kch, qm_scr[...], NT_DIMS, preferred_element_type=F32)
        sc = jnp.zeros((kc, tq), F32)
        for h in range(IDX_HEADS):
            sc = sc + wt_ref[0, h:h + 1, :] * jnp.maximum(r[:, h * tq:(h + 1) * tq], 0.0)
        if masked:
            kidx = start + lax.broadcasted_iota(I32, (kc, tq), 0)
            sc = jnp.where(kidx <= t_row, sc, -jnp.inf)
        key_scr[pl.ds(start, kc), :] = sc

    def score_body(c, _):
        score_chunk(c, False)
        return 0

    lax.fori_loop(0, n_chunks - 1, score_body, 0)
    score_chunk(n_chunks - 1, True)
    key_scr[pl.ds(pl.multiple_of(n_chunks * kc, kc), kc), :] = jnp.full((kc, tq), -jnp.inf, F32)

    def as_float(ordered):
        return pltpu.bitcast(jnp.where(ordered >= 0, ordered, ordered ^ 0x7FFFFFFF), F32)

    def count(pred):
        def body(t, cnt):
            start = pl.multiple_of(t * ks, ks)
            hit = jnp.where(pred(key_scr[pl.ds(start, ks), :], start), 1, 0)
            parts = [hit[r * 8:(r + 1) * 8, :] for r in range(ks // 8)]
            while len(parts) > 1:
                parts = [a + b for a, b in zip(parts[0::2], parts[1::2])]
            return cnt + parts[0]

        cnt8 = lax.fori_loop(0, n_trips, body, jnp.zeros((8, tq), I32))
        return jnp.sum(cnt8, axis=0, keepdims=True)

    def tau_body(b, ordered):
        cand = ordered + lax.shift_left(jnp.int32(1), 31 - b)
        cand_f = as_float(cand)
        return jnp.where(count(lambda sc, start: sc >= cand_f) >= topk, cand, ordered)

    ordered = lax.fori_loop(0, 32, tau_body, jnp.full((1, tq), INT_MIN, I32))
    tau = jnp.where(ordered == INT_MIN, jnp.finfo(F32).min, as_float(ordered))
    tau_scr[...] = tau
    cnt_ge = count(lambda sc, start: sc >= tau)

    @pl.when(jnp.max(cnt_ge) > topk)
    def _():
        cnt_gt = count(lambda sc, start: sc > tau)

        def cut_body(b, cut):
            cand = cut + lax.shift_left(jnp.int32(1), idx_bits - 1 - b)

            def pred(sc, start):
                kidx = start + lax.broadcasted_iota(I32, (ks, tq), 0)
                return (sc == tau) & (kidx < cand)

            return jnp.where(cnt_gt + count(pred) < topk, cand, cut)

        cut = lax.fori_loop(0, idx_bits, cut_body, jnp.zeros((1, tq), I32))

        def drop_body(t, _):
            start = pl.multiple_of(t * ks, ks)
            sc = key_scr[pl.ds(start, ks), :]
            kidx = start + lax.broadcasted_iota(I32, (ks, tq), 0)
            key_scr[pl.ds(start, ks), :] = jnp.where((sc == tau) & (kidx > cut), -jnp.inf, sc)
            return 0

        lax.fori_loop(0, n_trips, drop_body, 0)

    m_scr[...] = jnp.full(m_scr.shape, NEG_BIG, F32)
    l_scr[...] = jnp.zeros(l_scr.shape, F32)
    acc_scr[...] = jnp.zeros(acc_scr.shape, F32)
    c2 = scale * LOG2E

    def logits(c):
        kd = kd_ref[0, pl.ds(pl.multiple_of(c * kc, kc), kc), :]
        return lax.dot_general(kd, qa_scr[...], NT_DIMS, preferred_element_type=F32)

    def weighted_values(c):
        vt = vt_ref[0, :, pl.ds(pl.multiple_of(c * kc, kc), kc)]
        acc_scr[...] = alpha_scr[...] * acc_scr[...] + jnp.dot(
            vt, p_scr[...], preferred_element_type=F32)

    s_scr[...] = logits(0)
    p_scr[...] = jnp.zeros(p_scr.shape, BF16)
    alpha_scr[...] = jnp.ones(alpha_scr.shape, F32)

    def att_body(c, _):
        weighted_values(jnp.maximum(c - 1, 0))
        nxt = logits(jnp.minimum(c + 1, n_chunks - 1))
        sel = key_scr[pl.ds(pl.multiple_of(c * kc, kc), kc), :] >= tau_scr[...]
        s = s_scr[...]
        s = jnp.concatenate([jnp.where(sel, s[:, h * tq:(h + 1) * tq], NEG_BIG)
                             for h in range(n_dsa)], axis=1)
        m_old = m_scr[...]
        m_new = jnp.maximum(m_old, jnp.max(s, axis=0, keepdims=True))
        alpha = jnp.exp2((m_old - m_new) * c2)
        p = jnp.exp2((s - m_new) * c2)
        l_scr[...] = alpha * l_scr[...] + jnp.sum(p, axis=0, keepdims=True)
        m_scr[...] = m_new
        alpha_scr[...] = alpha
        p_scr[...] = p.astype(BF16)
        s_scr[...] = nxt
        return 0

    lax.fori_loop(0, n_chunks, att_body, 0)
    weighted_values(n_chunks - 1)

    out_t = acc_scr[...] / l_scr[...]
    for h in range(n_dsa):
        o_ref[0, :, h * HEAD_DIM:(h + 1) * HEAD_DIM] = out_t[:, h * tq:(h + 1) * tq].T


def _dsa(proj3, v_t, w_t, batch, seq, n_dsa, scale, tq=128, kc=256):
    width = n_dsa * HEAD_DIM
    topk = min(TOPK_MAX, seq // 4)
    idx_bits = int(np.ceil(np.log2(seq)))
    kern = functools.partial(_dsa_kernel, scale=scale, topk=topk, kc=kc, idx_bits=idx_bits)
    assert width == 1024 and IDX_HEADS * IDX_DIM == 1024
    misc = 5 * 1024 // LANES
    return pl.pallas_call(
        kern,
        out_shape=jax.ShapeDtypeStruct((batch, seq, width), F32),
        grid=(batch, seq // tq),
        in_specs=[
            pl.BlockSpec((1, tq, width), lambda b, i: (b, i, 3)),
            pl.BlockSpec((1, tq, IDX_HEADS * IDX_DIM), lambda b, i: (b, i, 4)),
            pl.BlockSpec((1, seq, HEAD_DIM), lambda b, i: (b, 0, misc)),
            pl.BlockSpec((1, HEAD_DIM, seq), lambda b, i: (b, 0, 0)),
            pl.BlockSpec((1, seq, LANES), lambda b, i: (b, 0, misc + 2)),
            pl.BlockSpec((1, IDX_HEADS, tq), lambda b, i: (b, 0, i)),
        ],
        out_specs=pl.BlockSpec((1, tq, width), lambda b, i: (b, i, 0)),
        scratch_shapes=[
            pltpu.VMEM((IDX_HEADS * tq, LANES), BF16),
            pltpu.VMEM((n_dsa * tq, HEAD_DIM), BF16),
            pltpu.VMEM((seq + kc, tq), F32),
            pltpu.VMEM((HEAD_DIM, n_dsa * tq), F32),
            pltpu.VMEM((1, n_dsa * tq), F32),
            pltpu.VMEM((1, n_dsa * tq), F32),
            pltpu.VMEM((1, tq), F32),
            pltpu.VMEM((kc, n_dsa * tq), F32),
            pltpu.VMEM((kc, n_dsa * tq), BF16),
            pltpu.VMEM((1, n_dsa * tq), F32),
        ],
        compiler_params=_cparams(("arbitrary", "arbitrary")),
        name="dsa",
    )(proj3, proj3, proj3, v_t, proj3, w_t)


def _rms(v, g):
    return v * lax.rsqrt(jnp.mean(v * v, axis=-1, keepdims=True) + EPS) * g


def _outp_kernel(osb_ref, ods_ref, x_ref, mod_ref, gsb_ref, gds_ref, g2_ref, w_ref,
                 x1_ref, h2_ref):
    half = osb_ref.shape[1]
    a = _rms(osb_ref[...], gsb_ref[...]).astype(BF16)
    b = _rms(ods_ref[...], gds_ref[...]).astype(BF16)
    y = (jnp.dot(a, w_ref[0:half, :], preferred_element_type=F32)
         + jnp.dot(b, w_ref[half:, :], preferred_element_type=F32))
    x1 = x_ref[...] + mod_ref[0, 2:3, :] * y
    x1_ref[...] = x1
    h2 = _rms(x1, g2_ref[...]) * (1.0 + mod_ref[0, 4:5, :]) + mod_ref[0, 3:4, :]
    h2_ref[...] = h2.astype(BF16)


def _outp(o_sb, o_ds, x2d, mod3, g_sb, g_ds, g2, w_out, seq, tm=512):
    rows, d = x2d.shape
    half = o_sb.shape[1]
    tiles_per_batch = seq // tm
    return pl.pallas_call(
        _outp_kernel,
        out_shape=(jax.ShapeDtypeStruct((rows, d), F32), jax.ShapeDtypeStruct((rows, d), BF16)),
        grid=(rows // tm,),
        in_specs=[
            pl.BlockSpec((tm, half), lambda i: (i, 0)),
            pl.BlockSpec((tm, half), lambda i: (i, 0)),
            pl.BlockSpec((tm, d), lambda i: (i, 0)),
            pl.BlockSpec((1, 6, d), lambda i: (i // tiles_per_batch, 0, 0)),
            pl.BlockSpec((1, half), lambda i: (0, 0)),
            pl.BlockSpec((1, half), lambda i: (0, 0)),
            pl.BlockSpec((1, d), lambda i: (0, 0)),
            pl.BlockSpec((2 * half, d), lambda i: (0, 0)),
        ],
        out_specs=(pl.BlockSpec((tm, d), lambda i: (i, 0)), pl.BlockSpec((tm, d), lambda i: (i, 0))),
        compiler_params=_cparams(("arbitrary",)),
        name="outp",
    )(o_sb, o_ds, x2d, mod3, g_sb, g_ds, g2, w_out)


def _ffn_kernel(h_ref, halo_ref, x1_ref, mod_ref, wg_ref, wv_ref, cwg_ref, cwv_ref,
                cbg_ref, cbv_ref, wd_ref, gf_ref, o_ref, *, tiles_per_batch, final_norm):
    i = pl.program_id(0)
    j = pl.program_id(1)
    tm = h_ref.shape[0]
    pad = halo_ref.shape[0]
    h = h_ref[...]
    halo = jnp.where(i % tiles_per_batch == 0, jnp.zeros_like(halo_ref[...]), halo_ref[...])

    row8 = lax.broadcasted_iota(I32, (8, 1), 0)

    def conv(w_ref, cw_ref, cb_ref, cols):
        w = w_ref[:, cols]
        u = jnp.dot(h, w, preferred_element_type=F32)
        tail = jnp.dot(halo, w, preferred_element_type=F32)[pad - 8:, :]
        uc = cb_ref[:, cols] + u * cw_ref[CONV_WIDTH - 1:CONV_WIDTH, cols]
        for tap in range(CONV_WIDTH - 1):
            back = CONV_WIDTH - 1 - tap
            rolled = pltpu.roll(u, back, axis=0)
            head = jnp.where(row8 < back, pltpu.roll(tail, back, axis=0), rolled[:8, :])
            shifted = jnp.concatenate([head, rolled[8:, :]], axis=0)
            uc = uc + shifted * cw_ref[tap:tap + 1, cols]
        return uc

    tn = wg_ref.shape[1]
    part = jnp.where(j > 0, o_ref[...], 0.0)
    for cols in (slice(0, tn // 2), slice(tn // 2, tn)):
        gate = conv(wg_ref, cwg_ref, cbg_ref, cols)
        val = conv(wv_ref, cwv_ref, cbv_ref, cols)
        a = (gate * jax.nn.sigmoid(gate) * val).astype(BF16)
        part = part + jnp.dot(a, wd_ref[cols, :], preferred_element_type=F32)
    o_ref[...] = part

    @pl.when(j == pl.num_programs(1) - 1)
    def _():
        x2 = x1_ref[...] + mod_ref[0, 5:6, :] * o_ref[...]
        o_ref[...] = _rms(x2, gf_ref[...]) if final_norm else x2


def _ffn(h2, x1, mod3, w_up_p, cw_g, cw_v, cb_g, cb_v, w_dn, g_f, seq, final_norm,
         tm=1024, tn=512):
    rows, d = x1.shape
    dff = w_up_p.shape[1] // 2
    tiles_per_batch = seq // tm
    pad = 16
    kern = functools.partial(_ffn_kernel, tiles_per_batch=tiles_per_batch, final_norm=final_norm)
    return pl.pallas_call(
        kern,
        out_shape=jax.ShapeDtypeStruct((rows, d), F32),
        grid=(rows // tm, dff // tn),
        in_specs=[
            pl.BlockSpec((tm, d), lambda i, j: (i, 0), pipeline_mode=pl.Buffered(1)),
            pl.BlockSpec((pad, d), lambda i, j: (jnp.maximum(i * (tm // pad) - 1, 0), 0)),
            pl.BlockSpec((tm, d), lambda i, j: (i, 0), pipeline_mode=pl.Buffered(1)),
            pl.BlockSpec((1, 6, d), lambda i, j: (i // tiles_per_batch, 0, 0)),
            pl.BlockSpec((d, tn), lambda i, j: (0, j)),
            pl.BlockSpec((d, tn), lambda i, j: (0, j + dff // tn)),
            pl.BlockSpec((CONV_WIDTH, tn), lambda i, j: (0, j)),
            pl.BlockSpec((CONV_WIDTH, tn), lambda i, j: (0, j)),
            pl.BlockSpec((1, tn), lambda i, j: (0, j)),
            pl.BlockSpec((1, tn), lambda i, j: (0, j)),
            pl.BlockSpec((tn, d), lambda i, j: (j, 0)),
            pl.BlockSpec((1, d), lambda i, j: (0, 0)),
        ],
        out_specs=pl.BlockSpec((tm, d), lambda i, j: (i, 0), pipeline_mode=pl.Buffered(1)),
        compiler_params=_cparams(("arbitrary", "arbitrary")),
        name="ffn",
    )(h2, h2, x1, mod3, w_up_p, w_up_p, cw_g, cw_v, cb_g, cb_v, w_dn, g_f)


def _pad_cols(a, n):
    return jnp.pad(a, ((0, 0), (0, n - a.shape[1])))


def _gather_cols_kernel(tbl_ref, x_ref, o_ref, *, n_plain):
    j = pl.program_id(0)
    x = x_ref[...]
    lane = lax.broadcasted_iota(I32, (1, LANES), 1)

    @pl.when(j < n_plain)
    def _():
        o_ref[...] = jnp.where(tbl_ref[j] >= 0, x, 0.0).astype(o_ref.dtype)

    @pl.when(j == n_plain)
    def _():
        o_ref[...] = jnp.where(lane < IDX_DIM, x, pltpu.roll(x, IDX_DIM, axis=1)).astype(o_ref.dtype)

    @pl.when(j == n_plain + 1)
    def _():
        o_ref[...] = jnp.where(lane < IDX_HEADS, pltpu.roll(x, IDX_DIM, axis=1), 0.0).astype(o_ref.dtype)


def _gather_col_blocks(w, table, name, n_plain=None):
    d = w.shape[0]
    n = len(table)
    kern = functools.partial(_gather_cols_kernel, n_plain=n if n_plain is None else n_plain)
    return pl.pallas_call(
        kern,
        out_shape=jax.ShapeDtypeStruct((d, n * LANES), BF16),
        grid_spec=pltpu.PrefetchScalarGridSpec(
            num_scalar_prefetch=1,
            grid=(n,),
            in_specs=[pl.BlockSpec((d, LANES), lambda j, t: (0, jnp.maximum(t[j], 0)))],
            out_specs=pl.BlockSpec((d, LANES), lambda j, t: (0, j)),
        ),
        compiler_params=_cparams(("arbitrary",)),
        name=name,
    )(jnp.asarray(table, I32), w)


def _pad_rows_kernel(x_ref, o_ref, *, valid):
    tr = x_ref.shape[0]
    row = pl.program_id(0) * tr + lax.broadcasted_iota(I32, (tr, 1), 0)
    o_ref[...] = jnp.where(row < valid, x_ref[...], 0.0).astype(o_ref.dtype)


def _pad_rows_bf16(w, n_rows, tr=512):
    valid, d = w.shape
    return pl.pallas_call(
        functools.partial(_pad_rows_kernel, valid=valid),
        out_shape=jax.ShapeDtypeStruct((n_rows, d), BF16),
        grid=(n_rows // tr,),
        in_specs=[pl.BlockSpec((tr, d), lambda j: (j, 0))],
        out_specs=pl.BlockSpec((tr, d), lambda j: (j, 0)),
        compiler_params=_cparams(("arbitrary",)),
        name="prep_w_down",
    )(w)


def _w_in_table(d_model):
    blk = d_model // 2 // LANES
    q_sb, k_sb, v_sb, q_ds = (list(range(g * blk, (g + 1) * blk)) for g in range(4))
    k_ds, v_ds = 4 * blk, 4 * blk + 1
    q_ix = list(range(4 * blk + 2, 4 * blk + 2 + IDX_HEADS * IDX_DIM // LANES))
    tail = q_ix[-1] + 1
    return q_sb + k_sb + v_sb + q_ds + q_ix + [k_ds, v_ds, tail, tail]


def kernel(x, c, positions, w_ada, b_ada, norm1_g, w_in, sb_norm_g, dsa_norm_g, w_out, norm2_g,
           w_up, conv_w, conv_b, w_down, final_norm_g):
    batch, seq, d = x.shape
    depth = w_ada.shape[0]
    n_sb = (d // HEAD_DIM) // 2
    n_dsa = d // HEAD_DIM - n_sb
    d_ff = w_down.shape[1]
    ff_pad = -(-d_ff // 512) * 512
    idx_scale = (IDX_HEADS ** -0.5) * (IDX_DIM ** -0.5)
    scale = HEAD_DIM ** -0.5
    rows = batch * seq

    half128 = HEAD_DIM // ROPE_FRACTION // 2
    half64 = IDX_DIM // ROPE_FRACTION // 2
    f128 = ROPE_THETA ** (-jnp.arange(half128, dtype=F32) / half128)
    f64 = ROPE_THETA ** (-jnp.arange(half64, dtype=F32) / half64)
    invf = jnp.zeros((8, LANES), F32)
    invf = invf.at[0].set(jnp.tile(f128, LANES // half128)).at[1].set(jnp.tile(f64, LANES // half64))

    tri_np = np.zeros((2 * SUB, 2 * SUB), np.float32)
    later = (np.arange(SUB)[:, None] > np.arange(SUB)[None, :]).astype(np.float32)
    tri_np[:SUB, :SUB] = later
    tri_np[SUB:, :SUB] = later
    tri_np[:, SUB:] = 1.0
    tri = jnp.asarray(tri_np, BF16)

    pos_col = positions.reshape(rows, 1)
    c_pad = jnp.pad(c, ((0, 16 - batch), (0, 0)))
    x2d = x.reshape(rows, d)

    for l in range(depth):
        mod = _ada(c_pad, w_ada[l], b_ada[l][None, :])[:batch]
        mod3 = mod.reshape(batch, 6, d)

        tbl = _w_in_table(d)
        w_in_p = _gather_col_blocks(w_in[l], tbl, "prep_w_in", n_plain=len(tbl) - 2)
        proj, v_t, w_t = _proj(x2d, mod3, norm1_g[l][None, :], pos_col, invf, w_in_p,
                               batch, seq, idx_scale)
        proj3 = proj.reshape(batch, seq, proj.shape[1])

        o_sb = _sb(proj3, tri, batch, seq, n_sb, scale)
        o_ds = _dsa(proj3, v_t, w_t, batch, seq, n_dsa, scale)

        x1, h2 = _outp(o_sb.reshape(rows, -1), o_ds.reshape(rows, -1), x2d, mod3,
                       sb_norm_g[l][None, :], dsa_norm_g[l][None, :], norm2_g[l][None, :],
                       w_out[l].astype(BF16), seq)

        n_ff, n_pad = d_ff // LANES, ff_pad // LANES
        gate_tbl = list(range(n_ff)) + [-1] * (n_pad - n_ff)
        w_up_p = _gather_col_blocks(
            w_up[l], gate_tbl + [b + n_ff if b >= 0 else -1 for b in gate_tbl], "prep_w_up")
        cw_g = _pad_cols(conv_w[l][:, :d_ff], ff_pad)
        cw_v = _pad_cols(conv_w[l][:, d_ff:], ff_pad)
        cb_g = _pad_cols(conv_b[l][None, :d_ff], ff_pad)
        cb_v = _pad_cols(conv_b[l][None, d_ff:], ff_pad)
        w_dn = _pad_rows_bf16(w_down[l], ff_pad)
        x2d = _ffn(h2, x1, mod3, w_up_p, cw_g, cw_v, cb_g, cb_v, w_dn,
                   final_norm_g[None, :], seq, final_norm=(l == depth - 1))
    return x2d.reshape(batch, seq, d)
```

```python
import functools

import jax
import jax.numpy as jnp
import numpy as np
from jax import lax
from jax.experimental import pallas as pl
from jax.experimental.pallas import tpu as pltpu

F32 = jnp.float32
BF16 = jnp.bfloat16
I32 = jnp.int32

HEAD_DIM = 128
IDX_HEADS = 16
IDX_DIM = 64
TOPK_MAX = 256
ROPE_THETA = 500000.0
ROPE_FRACTION = 4
CONV_WIDTH = 3
EPS = 1e-6
NEG_BIG = -1e30
INT_MIN = -(2 ** 31)
LOG2E = 1.4426950408889634

LANES = 128
VMEM_LIMIT = 56 * 1024 * 1024

NT_DIMS = (((1,), (1,)), ((), ()))


def _cparams(sem):
    return pltpu.CompilerParams(dimension_semantics=sem, vmem_limit_bytes=VMEM_LIMIT)


def _ada_kernel(c_ref, w_ref, b_ref, o_ref):
    c = c_ref[...]
    s = (c * jax.nn.sigmoid(c)).astype(BF16)
    o_ref[...] = jnp.dot(s, w_ref[...].astype(BF16), preferred_element_type=F32) + b_ref[...]


def _ada(c_pad, w_ada, b_ada, tn=1024):
    m, d = c_pad.shape
    n = w_ada.shape[1]
    return pl.pallas_call(
        _ada_kernel,
        out_shape=jax.ShapeDtypeStruct((m, n), F32),
        grid=(n // tn,),
        in_specs=[
            pl.BlockSpec((m, d), lambda j: (0, 0)),
            pl.BlockSpec((d, tn), lambda j: (0, j)),
            pl.BlockSpec((1, tn), lambda j: (0, j)),
        ],
        out_specs=pl.BlockSpec((m, tn), lambda j: (0, j)),
        compiler_params=_cparams(("arbitrary",)),
        name="ada",
    )(c_pad, w_ada, b_ada)


def _rope(xb, c, sn, sp, shift):
    return (xb * c + pltpu.roll(xb, LANES - shift, axis=1) * sn
            + pltpu.roll(xb, shift, axis=1) * sp)


def _proj_kernel(x_ref, mod_ref, g_ref, pos_ref, invf_ref, w_ref,
                 o_ref, vt_ref, wt_ref, h_scr, tab_scr, *, idx_scale):
    j = pl.program_id(1)
    tn = o_ref.shape[1]
    nsub = tn // LANES

    @pl.when(j == 0)
    def _():
        x = x_ref[...]
        ms = jnp.mean(x * x, axis=-1, keepdims=True)
        y = x * lax.rsqrt(ms + EPS) * g_ref[...]
        h = y * (1.0 + mod_ref[0, 1:2, :]) + mod_ref[0, 0:1, :]
        h_scr[...] = h.astype(BF16)
        pos = pos_ref[...].astype(F32)
        lane = lax.broadcasted_iota(I32, (1, LANES), 1)
        a128 = pos * invf_ref[0:1, :]
        c128, s128 = jnp.cos(a128), jnp.sin(a128)
        tab_scr[0] = jnp.where(lane < 32, c128, 1.0)
        tab_scr[1] = jnp.where(lane < 16, -s128, 0.0)
        tab_scr[2] = jnp.where((lane >= 16) & (lane < 32), s128, 0.0)
        l64 = lane & 63
        a64 = pos * invf_ref[1:2, :]
        c64, s64 = jnp.cos(a64), jnp.sin(a64)
        tab_scr[3] = jnp.where(l64 < 16, c64, 1.0)
        tab_scr[4] = jnp.where(l64 < 8, -s64, 0.0)
        tab_scr[5] = jnp.where((l64 >= 8) & (l64 < 16), s64, 0.0)

    acc = lax.dot_general(h_scr[...], w_ref[...], NT_DIMS, preferred_element_type=F32)

    def sub(s):
        return acc[:, s * LANES:(s + 1) * LANES]

    def rope128(xb):
        return _rope(xb, tab_scr[0], tab_scr[1], tab_scr[2], 16)

    def rope64(xb):
        return _rope(xb, tab_scr[3], tab_scr[4], tab_scr[5], 8)

    def store(s, val):
        o_ref[:, s * LANES:(s + 1) * LANES] = val.astype(BF16)

    @pl.when(j < 6)
    def _():
        o_ref[...] = acc.astype(BF16)

    @pl.when((j == 6) | (j == 7))
    def _():
        for s in range(nsub):
            store(s, rope128(sub(s)))

    @pl.when((j == 8) | (j == 9))
    def _():
        for s in range(nsub):
            store(s, rope64(sub(s)))

    @pl.when(j == 10)
    def _():
        store(0, rope128(sub(0)))
        v = sub(1)
        store(1, v)
        vt_ref[0] = v.T.astype(BF16)
        store(2, rope64(sub(2)))
        w = sub(3)
        store(3, w)
        wt_ref[0] = (w * idx_scale).T[0:IDX_HEADS, :]


def _proj(x2d, mod3, g1, pos_col, invf, w_in_p, batch, seq, idx_scale, tm=1024, tn=512):
    rows, d = x2d.shape
    ncols = w_in_p.shape[0]
    tiles_per_batch = seq // tm
    kern = functools.partial(_proj_kernel, idx_scale=idx_scale)
    return pl.pallas_call(
        kern,
        out_shape=(
            jax.ShapeDtypeStruct((rows, ncols), BF16),
            jax.ShapeDtypeStruct((batch, HEAD_DIM, seq), BF16),
            jax.ShapeDtypeStruct((batch, IDX_HEADS, seq), F32),
        ),
        grid=(rows // tm, ncols // tn),
        in_specs=[
            pl.BlockSpec((tm, d), lambda i, j: (i, 0)),
            pl.BlockSpec((1, 6, d), lambda i, j: (i // tiles_per_batch, 0, 0)),
            pl.BlockSpec((1, d), lambda i, j: (0, 0)),
            pl.BlockSpec((tm, 1), lambda i, j: (i, 0)),
            pl.BlockSpec((8, LANES), lambda i, j: (0, 0)),
            pl.BlockSpec((tn, d), lambda i, j: (j, 0)),
        ],
        out_specs=(
            pl.BlockSpec((tm, tn), lambda i, j: (i, j)),
            pl.BlockSpec((1, HEAD_DIM, tm),
                         lambda i, j: (i // tiles_per_batch, 0, i % tiles_per_batch)),
            pl.BlockSpec((1, IDX_HEADS, tm),
                         lambda i, j: (i // tiles_per_batch, 0, i % tiles_per_batch)),
        ),
        scratch_shapes=[
            pltpu.VMEM((tm, d), BF16),
            pltpu.VMEM((6, tm, LANES), F32),
        ],
        compiler_params=_cparams(("arbitrary", "arbitrary")),
        name="proj",
    )(x2d, mod3, g1, pos_col, invf, w_in_p)


SUB = 128


def _sb_kernel(q_ref, k_ref, v_ref, tri_ref, o_ref, carry_scr, acc_scr, *, scale, kc):
    i = pl.program_id(2)
    tq = q_ref.shape[1]
    nsub = kc // SUB
    n_diag = tq // kc
    q = q_ref[0]
    tri = tri_ref[...]
    t_row = i * tq + lax.broadcasted_iota(I32, (tq, kc), 0)

    def chunk(j, diag):
        start = pl.multiple_of(j * kc, kc)
        k = k_ref[0, pl.ds(start, kc), :]
        v = v_ref[0, pl.ds(start, kc), :]
        zn = lax.dot_general(q, k, NT_DIMS, preferred_element_type=F32) * (-scale)
        sp = jnp.log(1.0 + jnp.exp2(jnp.abs(zn) * (-LOG2E)))
        lneg = jnp.minimum(zn, 0.0) - sp
        lpos = lneg - zn
        if diag:
            strict = (start + lax.broadcasted_iota(I32, (tq, kc), 1)) < t_row
            lneg = jnp.where(strict, lneg, 0.0)
        hi = lneg.astype(BF16)
        lo = (lneg - hi.astype(F32)).astype(BF16)
        carry = carry_scr[...]
        parts = [None] * nsub
        for s in reversed(range(nsub)):
            cols = slice(s * SUB, (s + 1) * SUB)
            r = jnp.dot(jnp.concatenate([hi[:, cols], lo[:, cols]], axis=1), tri,
                        preferred_element_type=F32)
            parts[s] = lpos[:, cols] + r[:, :SUB] + carry
            carry = carry + r[:, SUB:]
        carry_scr[...] = carry
        w = jnp.exp(jnp.concatenate(parts, axis=1))
        if diag:
            w = jnp.where(strict, w, 0.0)
        acc_scr[...] += jnp.dot(w.astype(BF16), v, preferred_element_type=F32)

    carry_scr[...] = jnp.zeros(carry_scr.shape, F32)
    acc_scr[...] = jnp.zeros(acc_scr.shape, F32)
    first = i * n_diag
    for n in range(n_diag):
        chunk(first + n_diag - 1 - n, True)

    def body(n, _):
        for u in range(n_diag):
            chunk(first - 1 - n_diag * n - u, False)
        return 0

    lax.fori_loop(0, i, body, 0)
    o_ref[0] = acc_scr[...]


def _sb(proj3, tri, batch, seq, n_heads, scale, tq=512, kc=256):
    kern = functools.partial(_sb_kernel, scale=scale, kc=kc)
    return pl.pallas_call(
        kern,
        out_shape=jax.ShapeDtypeStruct((batch, seq, n_heads * HEAD_DIM), F32),
        grid=(batch, n_heads, seq // tq),
        in_specs=[
            pl.BlockSpec((1, tq, HEAD_DIM), lambda b, h, i: (b, i, h)),
            pl.BlockSpec((1, seq, HEAD_DIM), lambda b, h, i: (b, 0, n_heads + h)),
            pl.BlockSpec((1, seq, HEAD_DIM), lambda b, h, i: (b, 0, 2 * n_heads + h)),
            pl.BlockSpec((2 * SUB, 2 * SUB), lambda b, h, i: (0, 0)),
        ],
        out_specs=pl.BlockSpec((1, tq, HEAD_DIM), lambda b, h, i: (b, i, h)),
        scratch_shapes=[pltpu.VMEM((tq, SUB), F32), pltpu.VMEM((tq, HEAD_DIM), F32)],
        compiler_params=_cparams(("arbitrary", "arbitrary", "arbitrary")),
        name="sb",
    )(proj3, proj3, proj3, tri)


def _dsa_kernel(qd_ref, qi_ref, kd_ref, vt_ref, ki_ref, wt_ref, o_ref,
                qm_scr, qa_scr, key_scr, acc_scr, m_scr, l_scr, tau_scr, s_scr, p_scr, alpha_scr,
                *, scale, topk, kc, idx_bits):
    i = pl.program_id(1)
    tq = qd_ref.shape[1]
    n_dsa = qd_ref.shape[2] // HEAD_DIM
    n_chunks = ((i + 1) * tq + kc - 1) // kc
    ks = 2 * kc
    n_trips = (n_chunks + 1) // 2
    t_row = i * tq + lax.broadcasted_iota(I32, (1, tq), 1)

    lane = lax.broadcasted_iota(I32, (tq, LANES), 1)
    for h in range(IDX_HEADS):
        blk = qi_ref[0, :, (h // 2) * LANES:(h // 2 + 1) * LANES]
        keep = (lane < IDX_DIM) if h % 2 == 0 else (lane >= IDX_DIM)
        qm_scr[h * tq:(h + 1) * tq, :] = jnp.where(keep, blk, jnp.zeros_like(blk))
    for h in range(n_dsa):
        qa_scr[h * tq:(h + 1) * tq, :] = qd_ref[0, :, h * HEAD_DIM:(h + 1) * HEAD_DIM]

    def score_chunk(c, masked):
        start = pl.multiple_of(c * kc, kc)
        kch = ki_ref[0, pl.ds(start, kc), :]
        r = lax.dot_general(kch, qm_scr[...], NT_DIMS, preferred_element_type=F32)
        sc = jnp.zeros((kc, tq), F32)
        for h in range(IDX_HEADS):
            sc = sc + wt_ref[0, h:h + 1, :] * jnp.maximum(r[:, h * tq:(h + 1) * tq], 0.0)
        if masked:
            kidx = start + lax.broadcasted_iota(I32, (kc, tq), 0)
            sc = jnp.where(kidx <= t_row, sc, -jnp.inf)
        key_scr[pl.ds(start, kc), :] = sc

    def score_body(c, _):
        score_chunk(c, False)
        return 0

    lax.fori_loop(0, n_chunks - 1, score_body, 0)
    score_chunk(n_chunks - 1, True)
    key_scr[pl.ds(pl.multiple_of(n_chunks * kc, kc), kc), :] = jnp.full((kc, tq), -jnp.inf, F32)

    def as_float(ordered):
        return pltpu.bitcast(jnp.where(ordered >= 0, ordered, ordered ^ 0x7FFFFFFF), F32)

    def count(pred):
        def body(t, cnt):
            start = pl.multiple_of(t * ks, ks)
            hit = jnp.where(pred(key_scr[pl.ds(start, ks), :], start), 1, 0)
            parts = [hit[r * 8:(r + 1) * 8, :] for r in range(ks // 8)]
            while len(parts) > 1:
                parts = [a + b for a, b in zip(parts[0::2], parts[1::2])]
            return cnt + parts[0]

        cnt8 = lax.fori_loop(0, n_trips, body, jnp.zeros((8, tq), I32))
        return jnp.sum(cnt8, axis=0, keepdims=True)

    def tau_body(b, ordered):
        cand = ordered + lax.shift_left(jnp.int32(1), 31 - b)
        cand_f = as_float(cand)
        return jnp.where(count(lambda sc, start: sc >= cand_f) >= topk, cand, ordered)

    ordered = lax.fori_loop(0, 32, tau_body, jnp.full((1, tq), INT_MIN, I32))
    tau = jnp.where(ordered == INT_MIN, jnp.finfo(F32).min, as_float(ordered))
    tau_scr[...] = tau
    cnt_ge = count(lambda sc, start: sc >= tau)

    @pl.when(jnp.max(cnt_ge) > topk)
    def _():
        cnt_gt = count(lambda sc, start: sc > tau)

        def cut_body(b, cut):
            cand = cut + lax.shift_left(jnp.int32(1), idx_bits - 1 - b)

            def pred(sc, start):
                kidx = start + lax.broadcasted_iota(I32, (ks, tq), 0)
                return (sc == tau) & (kidx < cand)

            return jnp.where(cnt_gt + count(pred) < topk, cand, cut)

        cut = lax.fori_loop(0, idx_bits, cut_body, jnp.zeros((1, tq), I32))

        def drop_body(t, _):
            start = pl.multiple_of(t * ks, ks)
            sc = key_scr[pl.ds(start, ks), :]
            kidx = start + lax.broadcasted_iota(I32, (ks, tq), 0)
            key_scr[pl.ds(start, ks), :] = jnp.where((sc == tau) & (kidx > cut), -jnp.inf, sc)
            return 0

        lax.fori_loop(0, n_trips, drop_body, 0)

    m_scr[...] = jnp.full(m_scr.shape, NEG_BIG, F32)
    l_scr[...] = jnp.zeros(l_scr.shape, F32)
    acc_scr[...] = jnp.zeros(acc_scr.shape, F32)
    c2 = scale * LOG2E

    def logits(c):
        kd = kd_ref[0, pl.ds(pl.multiple_of(c * kc, kc), kc), :]
        return lax.dot_general(kd, qa_scr[...], NT_DIMS, preferred_element_type=F32)

    def weighted_values(c):
        vt = vt_ref[0, :, pl.ds(pl.multiple_of(c * kc, kc), kc)]
        acc_scr[...] = alpha_scr[...] * acc_scr[...] + jnp.dot(
            vt, p_scr[...], preferred_element_type=F32)

    s_scr[...] = logits(0)
    p_scr[...] = jnp.zeros(p_scr.shape, BF16)
    alpha_scr[...] = jnp.ones(alpha_scr.shape, F32)

    def att_body(c, _):
        weighted_values(jnp.maximum(c - 1, 0))
        nxt = logits(jnp.minimum(c + 1, n_chunks - 1))
        sel = key_scr[pl.ds(pl.multiple_of(c * kc, kc), kc), :] >= tau_scr[...]
        s = s_scr[...]
        s = jnp.concatenate([jnp.where(sel, s[:, h * tq:(h + 1) * tq], NEG_BIG)
                             for h in range(n_dsa)], axis=1)
        m_old = m_scr[...]
        m_new = jnp.maximum(m_old, jnp.max(s, axis=0, keepdims=True))
        alpha = jnp.exp2((m_old - m_new) * c2)
        p = jnp.exp2((s - m_new) * c2)
        l_scr[...] = alpha * l_scr[...] + jnp.sum(p, axis=0, keepdims=True)
        m_scr[...] = m_new
        alpha_scr[...] = alpha
        p_scr[...] = p.astype(BF16)
        s_scr[...] = nxt
        return 0

    lax.fori_loop(0, n_chunks, att_body, 0)
    weighted_values(n_chunks - 1)

    out_t = acc_scr[...] / l_scr[...]
    for h in range(n_dsa):
        o_ref[0, :, h * HEAD_DIM:(h + 1) * HEAD_DIM] = out_t[:, h * tq:(h + 1) * tq].T


def _dsa(proj3, v_t, w_t, batch, seq, n_dsa, scale, tq=128, kc=256):
    width = n_dsa * HEAD_DIM
    topk = min(TOPK_MAX, seq // 4)
    idx_bits = int(np.ceil(np.log2(seq)))
    kern = functools.partial(_dsa_kernel, scale=scale, topk=topk, kc=kc, idx_bits=idx_bits)
    assert width == 1024 and IDX_HEADS * IDX_DIM == 1024
    misc = 5 * 1024 // LANES
    return pl.pallas_call(
        kern,
        out_shape=jax.ShapeDtypeStruct((batch, seq, width), F32),
        grid=(batch, seq // tq),
        in_specs=[
            pl.BlockSpec((1, tq, width), lambda b, i: (b, i, 3)),
            pl.BlockSpec((1, tq, IDX_HEADS * IDX_DIM), lambda b, i: (b, i, 4)),
            pl.BlockSpec((1, seq, HEAD_DIM), lambda b, i: (b, 0, misc)),
            pl.BlockSpec((1, HEAD_DIM, seq), lambda b, i: (b, 0, 0)),
            pl.BlockSpec((1, seq, LANES), lambda b, i: (b, 0, misc + 2)),
            pl.BlockSpec((1, IDX_HEADS, tq), lambda b, i: (b, 0, i)),
        ],
        out_specs=pl.BlockSpec((1, tq, width), lambda b, i: (b, i, 0)),
        scratch_shapes=[
            pltpu.VMEM((IDX_HEADS * tq, LANES), BF16),
            pltpu.VMEM((n_dsa * tq, HEAD_DIM), BF16),
            pltpu.VMEM((seq + kc, tq), F32),
            pltpu.VMEM((HEAD_DIM, n_dsa * tq), F32),
            pltpu.VMEM((1, n_dsa * tq), F32),
            pltpu.VMEM((1, n_dsa * tq), F32),
            pltpu.VMEM((1, tq), F32),
            pltpu.VMEM((kc, n_dsa * tq), F32),
            pltpu.VMEM((kc, n_dsa * tq), BF16),
            pltpu.VMEM((1, n_dsa * tq), F32),
        ],
        compiler_params=_cparams(("arbitrary", "arbitrary")),
        name="dsa",
    )(proj3, proj3, proj3, v_t, proj3, w_t)


def _rms(v, g):
    return v * lax.rsqrt(jnp.mean(v * v, axis=-1, keepdims=True) + EPS) * g


def _outp_kernel(osb_ref, ods_ref, x_ref, mod_ref, gsb_ref, gds_ref, g2_ref, w_ref,
                 x1_ref, h2_ref):
    half = osb_ref.shape[1]
    a = _rms(osb_ref[...], gsb_ref[...]).astype(BF16)
    b = _rms(ods_ref[...], gds_ref[...]).astype(BF16)
    y = (jnp.dot(a, w_ref[0:half, :], preferred_element_type=F32)
         + jnp.dot(b, w_ref[half:, :], preferred_element_type=F32))
    x1 = x_ref[...] + mod_ref[0, 2:3, :] * y
    x1_ref[...] = x1
    h2 = _rms(x1, g2_ref[...]) * (1.0 + mod_ref[0, 4:5, :]) + mod_ref[0, 3:4, :]
    h2_ref[...] = h2.astype(BF16)


def _outp(o_sb, o_ds, x2d, mod3, g_sb, g_ds, g2, w_out, seq, tm=512):
    rows, d = x2d.shape
    half = o_sb.shape[1]
    tiles_per_batch = seq // tm
    return pl.pallas_call(
        _outp_kernel,
        out_shape=(jax.ShapeDtypeStruct((rows, d), F32), jax.ShapeDtypeStruct((rows, d), BF16)),
        grid=(rows // tm,),
        in_specs=[
            pl.BlockSpec((tm, half), lambda i: (i, 0)),
            pl.BlockSpec((tm, half), lambda i: (i, 0)),
            pl.BlockSpec((tm, d), lambda i: (i, 0)),
            pl.BlockSpec((1, 6, d), lambda i: (i // tiles_per_batch, 0, 0)),
            pl.BlockSpec((1, half), lambda i: (0, 0)),
            pl.BlockSpec((1, half), lambda i: (0, 0)),
            pl.BlockSpec((1, d), lambda i: (0, 0)),
            pl.BlockSpec((2 * half, d), lambda i: (0, 0)),
        ],
        out_specs=(pl.BlockSpec((tm, d), lambda i: (i, 0)), pl.BlockSpec((tm, d), lambda i: (i, 0))),
        compiler_params=_cparams(("arbitrary",)),
        name="outp",
    )(o_sb, o_ds, x2d, mod3, g_sb, g_ds, g2, w_out)


def _ffn_kernel(h_ref, halo_ref, x1_ref, mod_ref, wg_ref, wv_ref, cwg_ref, cwv_ref,
                cbg_ref, cbv_ref, wd_ref, gf_ref, o_ref, *, tiles_per_batch, final_norm):
    i = pl.program_id(0)
    j = pl.program_id(1)
    tm = h_ref.shape[0]
    pad = halo_ref.shape[0]
    h = h_ref[...]
    halo = jnp.where(i % tiles_per_batch == 0, jnp.zeros_like(halo_ref[...]), halo_ref[...])

    row8 = lax.broadcasted_iota(I32, (8, 1), 0)

    def conv(w_ref, cw_ref, cb_ref, cols):
        w = w_ref[:, cols]
        u = jnp.dot(h, w, preferred_element_type=F32)
        tail = jnp.dot(halo, w, preferred_element_type=F32)[pad - 8:, :]
        uc = cb_ref[:, cols] + u * cw_ref[CONV_WIDTH - 1:CONV_WIDTH, cols]
        for tap in range(CONV_WIDTH - 1):
            back = CONV_WIDTH - 1 - tap
            rolled = pltpu.roll(u, back, axis=0)
            head = jnp.where(row8 < back, pltpu.roll(tail, back, axis=0), rolled[:8, :])
            shifted = jnp.concatenate([head, rolled[8:, :]], axis=0)
            uc = uc + shifted * cw_ref[tap:tap + 1, cols]
        return uc

    tn = wg_ref.shape[1]
    part = jnp.where(j > 0, o_ref[...], 0.0)
    for cols in (slice(0, tn // 2), slice(tn // 2, tn)):
        gate = conv(wg_ref, cwg_ref, cbg_ref, cols)
        val = conv(wv_ref, cwv_ref, cbv_ref, cols)
        a = (gate * jax.nn.sigmoid(gate) * val).astype(BF16)
        part = part + jnp.dot(a, wd_ref[cols, :], preferred_element_type=F32)
    o_ref[...] = part

    @pl.when(j == pl.num_programs(1) - 1)
    def _():
        x2 = x1_ref[...] + mod_ref[0, 5:6, :] * o_ref[...]
        o_ref[...] = _rms(x2, gf_ref[...]) if final_norm else x2


def _ffn(h2, x1, mod3, w_up_p, cw_g, cw_v, cb_g, cb_v, w_dn, g_f, seq, final_norm,
         tm=1024, tn=512):
    rows, d = x1.shape
    dff = w_up_p.shape[1] // 2
    tiles_per_batch = seq // tm
    pad = 16
    kern = functools.partial(_ffn_kernel, tiles_per_batch=tiles_per_batch, final_norm=final_norm)
    return pl.pallas_call(
        kern,
        out_shape=jax.ShapeDtypeStruct((rows, d), F32),
        grid=(rows // tm, dff // tn),
        in_specs=[
            pl.BlockSpec((tm, d), lambda i, j: (i, 0), pipeline_mode=pl.Buffered(1)),
            pl.BlockSpec((pad, d), lambda i, j: (jnp.maximum(i * (tm // pad) - 1, 0), 0)),
            pl.BlockSpec((tm, d), lambda i, j: (i, 0), pipeline_mode=pl.Buffered(1)),
            pl.BlockSpec((1, 6, d), lambda i, j: (i // tiles_per_batch, 0, 0)),
            pl.BlockSpec((d, tn), lambda i, j: (0, j)),
            pl.BlockSpec((d, tn), lambda i, j: (0, j + dff // tn)),
            pl.BlockSpec((CONV_WIDTH, tn), lambda i, j: (0, j)),
            pl.BlockSpec((CONV_WIDTH, tn), lambda i, j: (0, j)),
            pl.BlockSpec((1, tn), lambda i, j: (0, j)),
            pl.BlockSpec((1, tn), lambda i, j: (0, j)),
            pl.BlockSpec((tn, d), lambda i, j: (j, 0)),
            pl.BlockSpec((1, d), lambda i, j: (0, 0)),
        ],
        out_specs=pl.BlockSpec((tm, d), lambda i, j: (i, 0), pipeline_mode=pl.Buffered(1)),
        compiler_params=_cparams(("arbitrary", "arbitrary")),
        name="ffn",
    )(h2, h2, x1, mod3, w_up_p, w_up_p, cw_g, cw_v, cb_g, cb_v, w_dn, g_f)


def _pad_cols(a, n):
    return jnp.pad(a, ((0, 0), (0, n - a.shape[1])))


def _regroup_rows_kernel(tbl_ref, x_ref, o_ref, *, n_plain):
    j = pl.program_id(0)

    @pl.when(j < n_plain)
    def _():
        o_ref[...] = x_ref[...].astype(o_ref.dtype)

    @pl.when(j == n_plain)
    def _():
        k = x_ref[0:IDX_DIM, :].astype(o_ref.dtype)
        o_ref[0:IDX_DIM, :] = k
        o_ref[IDX_DIM:2 * IDX_DIM, :] = k

    @pl.when(j == n_plain + 1)
    def _():
        o_ref[...] = jnp.zeros(o_ref.shape, o_ref.dtype)
        o_ref[0:IDX_HEADS, :] = x_ref[IDX_DIM:IDX_DIM + IDX_HEADS, :].astype(o_ref.dtype)


def _regroup_w_in_t(w_t, table):
    d = w_t.shape[1]
    n = len(table)
    return pl.pallas_call(
        functools.partial(_regroup_rows_kernel, n_plain=n - 2),
        out_shape=jax.ShapeDtypeStruct((n * LANES, d), BF16),
        grid_spec=pltpu.PrefetchScalarGridSpec(
            num_scalar_prefetch=1,
            grid=(n,),
            in_specs=[pl.BlockSpec((LANES, d), lambda j, t: (t[j], 0))],
            out_specs=pl.BlockSpec((LANES, d), lambda j, t: (j, 0)),
        ),
        compiler_params=_cparams(("arbitrary",)),
        name="prep_w_in",
    )(jnp.asarray(table, I32), w_t)


def _pad_halves_kernel(x_ref, o_ref):
    valid = x_ref.shape[1]
    o_ref[:, 0:valid] = x_ref[...].astype(o_ref.dtype)
    o_ref[:, valid:] = jnp.zeros((o_ref.shape[0], o_ref.shape[1] - valid), o_ref.dtype)


def _pad_halves_bf16(w, half, half_pad, tr=256):
    d = w.shape[0]
    return pl.pallas_call(
        _pad_halves_kernel,
        out_shape=jax.ShapeDtypeStruct((d, 2 * half_pad), BF16),
        grid=(d // tr, 2),
        in_specs=[pl.BlockSpec((tr, half), lambda i, g: (i, g))],
        out_specs=pl.BlockSpec((tr, half_pad), lambda i, g: (i, g)),
        compiler_params=_cparams(("arbitrary", "arbitrary")),
        name="prep_w_up",
    )(w)


def _pad_rows_kernel(x_ref, o_ref, *, valid):
    tr = x_ref.shape[0]
    row = pl.program_id(0) * tr + lax.broadcasted_iota(I32, (tr, 1), 0)
    o_ref[...] = jnp.where(row < valid, x_ref[...], 0.0).astype(o_ref.dtype)


def _pad_rows_bf16(w, n_rows, tr=512):
    valid, d = w.shape
    return pl.pallas_call(
        functools.partial(_pad_rows_kernel, valid=valid),
        out_shape=jax.ShapeDtypeStruct((n_rows, d), BF16),
        grid=(n_rows // tr,),
        in_specs=[pl.BlockSpec((tr, d), lambda j: (j, 0))],
        out_specs=pl.BlockSpec((tr, d), lambda j: (j, 0)),
        compiler_params=_cparams(("arbitrary",)),
        name="prep_w_down",
    )(w)


def _w_in_table(d_model):
    blk = d_model // 2 // LANES
    q_sb, k_sb, v_sb, q_ds = (list(range(g * blk, (g + 1) * blk)) for g in range(4))
    k_ds, v_ds = 4 * blk, 4 * blk + 1
    q_ix = list(range(4 * blk + 2, 4 * blk + 2 + IDX_HEADS * IDX_DIM // LANES))
    tail = q_ix[-1] + 1
    return q_sb + k_sb + v_sb + q_ds + q_ix + [k_ds, v_ds, tail, tail]


def kernel(x, c, positions, w_ada, b_ada, norm1_g, w_in, sb_norm_g, dsa_norm_g, w_out, norm2_g,
           w_up, conv_w, conv_b, w_down, final_norm_g):
    batch, seq, d = x.shape
    depth = w_ada.shape[0]
    n_sb = (d // HEAD_DIM) // 2
    n_dsa = d // HEAD_DIM - n_sb
    d_ff = w_down.shape[1]
    ff_pad = -(-d_ff // 512) * 512
    idx_scale = (IDX_HEADS ** -0.5) * (IDX_DIM ** -0.5)
    scale = HEAD_DIM ** -0.5
    rows = batch * seq

    half128 = HEAD_DIM // ROPE_FRACTION // 2
    half64 = IDX_DIM // ROPE_FRACTION // 2
    f128 = ROPE_THETA ** (-jnp.arange(half128, dtype=F32) / half128)
    f64 = ROPE_THETA ** (-jnp.arange(half64, dtype=F32) / half64)
    invf = jnp.zeros((8, LANES), F32)
    invf = invf.at[0].set(jnp.tile(f128, LANES // half128)).at[1].set(jnp.tile(f64, LANES // half64))

    tri_np = np.zeros((2 * SUB, 2 * SUB), np.float32)
    later = (np.arange(SUB)[:, None] > np.arange(SUB)[None, :]).astype(np.float32)
    tri_np[:SUB, :SUB] = later
    tri_np[SUB:, :SUB] = later
    tri_np[:, SUB:] = 1.0
    tri = jnp.asarray(tri_np, BF16)

    pos_col = positions.reshape(rows, 1)
    c_pad = jnp.pad(c, ((0, 16 - batch), (0, 0)))
    x2d = x.reshape(rows, d)

    for l in range(depth):
        mod = _ada(c_pad, w_ada[l], b_ada[l][None, :])[:batch]
        mod3 = mod.reshape(batch, 6, d)

        w_in_p = _regroup_w_in_t(jnp.swapaxes(w_in[l], 0, 1), _w_in_table(d))
        proj, v_t, w_t = _proj(x2d, mod3, norm1_g[l][None, :], pos_col, invf, w_in_p,
                               batch, seq, idx_scale)
        proj3 = proj.reshape(batch, seq, proj.shape[1])

        o_sb = _sb(proj3, tri, batch, seq, n_sb, scale)
        o_ds = _dsa(proj3, v_t, w_t, batch, seq, n_dsa, scale)

        x1, h2 = _outp(o_sb.reshape(rows, -1), o_ds.reshape(rows, -1), x2d, mod3,
                       sb_norm_g[l][None, :], dsa_norm_g[l][None, :], norm2_g[l][None, :],
                       w_out[l].astype(BF16), seq)

        w_up_p = _pad_halves_bf16(w_up[l], d_ff, ff_pad)
        cw_g = _pad_cols(conv_w[l][:, :d_ff], ff_pad)
        cw_v = _pad_cols(conv_w[l][:, d_ff:], ff_pad)
        cb_g = _pad_cols(conv_b[l][None, :d_ff], ff_pad)
        cb_v = _pad_cols(conv_b[l][None, d_ff:], ff_pad)
        w_dn = _pad_rows_bf16(w_down[l], ff_pad)
        x2d = _ffn(h2, x1, mod3, w_up_p, cw_g, cw_v, cb_g, cb_v, w_dn,
                   final_norm_g[None, :], seq, final_norm=(l == depth - 1))
    return x2d.reshape(batch, seq, d)
```

```python
import functools

import jax
import jax.numpy as jnp
import numpy as np
from jax import lax
from jax.experimental import pallas as pl
from jax.experimental.pallas import tpu as pltpu

F32 = jnp.float32
BF16 = jnp.bfloat16
I32 = jnp.int32

HEAD_DIM = 128
IDX_HEADS = 16
IDX_DIM = 64
TOPK_MAX = 256
ROPE_THETA = 500000.0
ROPE_FRACTION = 4
CONV_WIDTH = 3
EPS = 1e-6
NEG_BIG = -1e30
INT_MIN = -(2 ** 31)
LOG2E = 1.4426950408889634

LANES = 128
VMEM_LIMIT = 56 * 1024 * 1024

NT_DIMS = (((1,), (1,)), ((), ()))


def _cparams(sem):
    return pltpu.CompilerParams(dimension_semantics=sem, vmem_limit_bytes=VMEM_LIMIT)


def _ada_kernel(c_ref, w_ref, b_ref, o_ref):
    c = c_ref[...]
    s = (c * jax.nn.sigmoid(c)).astype(BF16)
    o_ref[...] = jnp.dot(s, w_ref[...].astype(BF16), preferred_element_type=F32) + b_ref[...]


def _ada(c_pad, w_ada, b_ada, tn=1024):
    m, d = c_pad.shape
    n = w_ada.shape[1]
    return pl.pallas_call(
        _ada_kernel,
        out_shape=jax.ShapeDtypeStruct((m, n), F32),
        grid=(n // tn,),
        in_specs=[
            pl.BlockSpec((m, d), lambda j: (0, 0)),
            pl.BlockSpec((d, tn), lambda j: (0, j)),
            pl.BlockSpec((1, tn), lambda j: (0, j)),
        ],
        out_specs=pl.BlockSpec((m, tn), lambda j: (0, j)),
        compiler_params=_cparams(("arbitrary",)),
        name="ada",
    )(c_pad, w_ada, b_ada)


def _rope(xb, c, sn, sp, shift):
    return (xb * c + pltpu.roll(xb, LANES - shift, axis=1) * sn
            + pltpu.roll(xb, shift, axis=1) * sp)


def _proj_kernel(x_ref, mod_ref, g_ref, pos_ref, invf_ref, w_ref,
                 o_ref, vt_ref, wt_ref, h_scr, tab_scr, *, idx_scale):
    j = pl.program_id(1)
    tn = o_ref.shape[1]
    nsub = tn // LANES

    @pl.when(j == 0)
    def _():
        x = x_ref[...]
        ms = jnp.mean(x * x, axis=-1, keepdims=True)
        y = x * lax.rsqrt(ms + EPS) * g_ref[...]
        h = y * (1.0 + mod_ref[0, 1:2, :]) + mod_ref[0, 0:1, :]
        h_scr[...] = h.astype(BF16)
        pos = pos_ref[...].astype(F32)
        lane = lax.broadcasted_iota(I32, (1, LANES), 1)
        a128 = pos * invf_ref[0:1, :]
        c128, s128 = jnp.cos(a128), jnp.sin(a128)
        tab_scr[0] = jnp.where(lane < 32, c128, 1.0)
        tab_scr[1] = jnp.where(lane < 16, -s128, 0.0)
        tab_scr[2] = jnp.where((lane >= 16) & (lane < 32), s128, 0.0)
        l64 = lane & 63
        a64 = pos * invf_ref[1:2, :]
        c64, s64 = jnp.cos(a64), jnp.sin(a64)
        tab_scr[3] = jnp.where(l64 < 16, c64, 1.0)
        tab_scr[4] = jnp.where(l64 < 8, -s64, 0.0)
        tab_scr[5] = jnp.where((l64 >= 8) & (l64 < 16), s64, 0.0)

    acc = lax.dot_general(h_scr[...], w_ref[...], NT_DIMS, preferred_element_type=F32)

    def sub(s):
        return acc[:, s * LANES:(s + 1) * LANES]

    def rope128(xb):
        return _rope(xb, tab_scr[0], tab_scr[1], tab_scr[2], 16)

    def rope64(xb):
        return _rope(xb, tab_scr[3], tab_scr[4], tab_scr[5], 8)

    def store(s, val):
        o_ref[:, s * LANES:(s + 1) * LANES] = val.astype(BF16)

    @pl.when(j < 6)
    def _():
        o_ref[...] = acc.astype(BF16)

    @pl.when((j == 6) | (j == 7))
    def _():
        for s in range(nsub):
            store(s, rope128(sub(s)))

    @pl.when((j == 8) | (j == 9))
    def _():
        for s in range(nsub):
            store(s, rope64(sub(s)))

    @pl.when(j == 10)
    def _():
        store(0, rope128(sub(0)))
        v = sub(1)
        store(1, v)
        vt_ref[0] = v.T.astype(BF16)
        store(2, rope64(sub(2)))
        w = sub(3)
        store(3, w)
        wt_ref[0] = (w * idx_scale).T[0:IDX_HEADS, :]


def _proj(x2d, mod3, g1, pos_col, invf, w_in_p, batch, seq, idx_scale, tm=1024, tn=512):
    rows, d = x2d.shape
    ncols = w_in_p.shape[0]
    tiles_per_batch = seq // tm
    kern = functools.partial(_proj_kernel, idx_scale=idx_scale)
    return pl.pallas_call(
        kern,
        out_shape=(
            jax.ShapeDtypeStruct((rows, ncols), BF16),
            jax.ShapeDtypeStruct((batch, HEAD_DIM, seq), BF16),
            jax.ShapeDtypeStruct((batch, IDX_HEADS, seq), F32),
        ),
        grid=(rows // tm, ncols // tn),
        in_specs=[
            pl.BlockSpec((tm, d), lambda i, j: (i, 0)),
            pl.BlockSpec((1, 6, d), lambda i, j: (i // tiles_per_batch, 0, 0)),
            pl.BlockSpec((1, d), lambda i, j: (0, 0)),
            pl.BlockSpec((tm, 1), lambda i, j: (i, 0)),
            pl.BlockSpec((8, LANES), lambda i, j: (0, 0)),
            pl.BlockSpec((tn, d), lambda i, j: (j, 0)),
        ],
        out_specs=(
            pl.BlockSpec((tm, tn), lambda i, j: (i, j)),
            pl.BlockSpec((1, HEAD_DIM, tm),
                         lambda i, j: (i // tiles_per_batch, 0, i % tiles_per_batch)),
            pl.BlockSpec((1, IDX_HEADS, tm),
                         lambda i, j: (i // tiles_per_batch, 0, i % tiles_per_batch)),
        ),
        scratch_shapes=[
            pltpu.VMEM((tm, d), BF16),
            pltpu.VMEM((6, tm, LANES), F32),
        ],
        compiler_params=_cparams(("arbitrary", "arbitrary")),
        name="proj",
    )(x2d, mod3, g1, pos_col, invf, w_in_p)


SUB = 128


def _sb_kernel(q_ref, k_ref, v_ref, tri_ref, o_ref, carry_scr, acc_scr, *, scale, kc):
    i = pl.program_id(2)
    tq = q_ref.shape[1]
    nsub = kc // SUB
    n_diag = tq // kc
    q = q_ref[0]
    tri = tri_ref[...]
    t_row = i * tq + lax.broadcasted_iota(I32, (tq, kc), 0)

    def chunk(j, diag):
        start = pl.multiple_of(j * kc, kc)
        k = k_ref[0, pl.ds(start, kc), :]
        v = v_ref[0, pl.ds(start, kc), :]
        zn = lax.dot_general(q, k, NT_DIMS, preferred_element_type=F32) * (-scale)
        sp = jnp.log(1.0 + jnp.exp2(jnp.abs(zn) * (-LOG2E)))
        lneg = jnp.minimum(zn, 0.0) - sp
        lpos = lneg - zn
        if diag:
            strict = (start + lax.broadcasted_iota(I32, (tq, kc), 1)) < t_row
            lneg = jnp.where(strict, lneg, 0.0)
        hi = lneg.astype(BF16)
        lo = (lneg - hi.astype(F32)).astype(BF16)
        carry = carry_scr[...]
        parts = [None] * nsub
        for s in reversed(range(nsub)):
            cols = slice(s * SUB, (s + 1) * SUB)
            r = jnp.dot(jnp.concatenate([hi[:, cols], lo[:, cols]], axis=1), tri,
                        preferred_element_type=F32)
            parts[s] = lpos[:, cols] + r[:, :SUB] + carry
            carry = carry + r[:, SUB:]
        carry_scr[...] = carry
        w = jnp.exp(jnp.concatenate(parts, axis=1))
        if diag:
            w = jnp.where(strict, w, 0.0)
        acc_scr[...] += jnp.dot(w.astype(BF16), v, preferred_element_type=F32)

    carry_scr[...] = jnp.zeros(carry_scr.shape, F32)
    acc_scr[...] = jnp.zeros(acc_scr.shape, F32)
    first = i * n_diag
    for n in range(n_diag):
        chunk(first + n_diag - 1 - n, True)

    def body(n, _):
        for u in range(n_diag):
            chunk(first - 1 - n_diag * n - u, False)
        return 0

    lax.fori_loop(0, i, body, 0)
    o_ref[0] = acc_scr[...]


def _sb(proj3, tri, batch, seq, n_heads, scale, tq=1024, kc=256):
    kern = functools.partial(_sb_kernel, scale=scale, kc=kc)
    return pl.pallas_call(
        kern,
        out_shape=jax.ShapeDtypeStruct((batch, seq, n_heads * HEAD_DIM), F32),
        grid=(batch, n_heads, seq // tq),
        in_specs=[
            pl.BlockSpec((1, tq, HEAD_DIM), lambda b, h, i: (b, i, h)),
            pl.BlockSpec((1, seq, HEAD_DIM), lambda b, h, i: (b, 0, n_heads + h)),
            pl.BlockSpec((1, seq, HEAD_DIM), lambda b, h, i: (b, 0, 2 * n_heads + h)),
            pl.BlockSpec((2 * SUB, 2 * SUB), lambda b, h, i: (0, 0)),
        ],
        out_specs=pl.BlockSpec((1, tq, HEAD_DIM), lambda b, h, i: (b, i, h)),
        scratch_shapes=[pltpu.VMEM((tq, SUB), F32), pltpu.VMEM((tq, HEAD_DIM), F32)],
        compiler_params=_cparams(("arbitrary", "arbitrary", "arbitrary")),
        name="sb",
    )(proj3, proj3, proj3, tri)


def _dsa_kernel(qd_ref, qi_ref, kd_ref, vt_ref, ki_ref, wt_ref, o_ref,
                qm_scr, qa_scr, key_scr, acc_scr, m_scr, l_scr, tau_scr, s_scr, p_scr, alpha_scr,
                *, scale, topk, kc, idx_bits):
    i = pl.program_id(1)
    tq = qd_ref.shape[1]
    n_dsa = qd_ref.shape[2] // HEAD_DIM
    n_chunks = ((i + 1) * tq + kc - 1) // kc
    ks = 2 * kc
    n_trips = (n_chunks + 1) // 2
    t_row = i * tq + lax.broadcasted_iota(I32, (1, tq), 1)

    lane = lax.broadcasted_iota(I32, (tq, LANES), 1)
    for h in range(IDX_HEADS):
        blk = qi_ref[0, :, (h // 2) * LANES:(h // 2 + 1) * LANES]
        keep = (lane < IDX_DIM) if h % 2 == 0 else (lane >= IDX_DIM)
        qm_scr[h * tq:(h + 1) * tq, :] = jnp.where(keep, blk, jnp.zeros_like(blk))
    for h in range(n_dsa):
        qa_scr[h * tq:(h + 1) * tq, :] = qd_ref[0, :, h * HEAD_DIM:(h + 1) * HEAD_DIM]

    def score_chunk(c, masked):
        start = pl.multiple_of(c * kc, kc)
        kch = ki_ref[0, pl.ds(start, kc), :]
        r = lax.dot_general(kch, qm_scr[...], NT_DIMS, preferred_element_type=F32)
        sc = jnp.zeros((kc, tq), F32)
        for h in range(IDX_HEADS):
            sc = sc + wt_ref[0, h:h + 1, :] * jnp.maximum(r[:, h * tq:(h + 1) * tq], 0.0)
        if masked:
            kidx = start + lax.broadcasted_iota(I32, (kc, tq), 0)
            sc = jnp.where(kidx <= t_row, sc, -jnp.inf)
        key_scr[pl.ds(start, kc), :] = sc

    def score_body(c, _):
        score_chunk(c, False)
        return 0

    lax.fori_loop(0, n_chunks - 1, score_body, 0)
    score_chunk(n_chunks - 1, True)
    key_scr[pl.ds(pl.multiple_of(n_chunks * kc, kc), kc), :] = jnp.full((kc, tq), -jnp.inf, F32)

    def as_float(ordered):
        return pltpu.bitcast(jnp.where(ordered >= 0, ordered, ordered ^ 0x7FFFFFFF), F32)

    def count(pred):
        def body(t, cnt):
            start = pl.multiple_of(t * ks, ks)
            hit = jnp.where(pred(key_scr[pl.ds(start, ks), :], start), 1, 0)
            parts = [hit[r * 8:(r + 1) * 8, :] for r in range(ks // 8)]
            while len(parts) > 1:
                parts = [a + b for a, b in zip(parts[0::2], parts[1::2])]
            return cnt + parts[0]

        cnt8 = lax.fori_loop(0, n_trips, body, jnp.zeros((8, tq), I32))
        return jnp.sum(cnt8, axis=0, keepdims=True)

    def tau_body(b, ordered):
        cand = ordered + lax.shift_left(jnp.int32(1), 31 - b)
        cand_f = as_float(cand)
        return jnp.where(count(lambda sc, start: sc >= cand_f) >= topk, cand, ordered)

    ordered = lax.fori_loop(0, 32, tau_body, jnp.full((1, tq), INT_MIN, I32))
    tau = jnp.where(ordered == INT_MIN, jnp.finfo(F32).min, as_float(ordered))
    tau_scr[...] = tau
    cnt_ge = count(lambda sc, start: sc >= tau)

    @pl.when(jnp.max(cnt_ge) > topk)
    def _():
        cnt_gt = count(lambda sc, start: sc > tau)

        def cut_body(b, cut):
            cand = cut + lax.shift_left(jnp.int32(1), idx_bits - 1 - b)

            def pred(sc, start):
                kidx = start + lax.broadcasted_iota(I32, (ks, tq), 0)
                return (sc == tau) & (kidx < cand)

            return jnp.where(cnt_gt + count(pred) < topk, cand, cut)

        cut = lax.fori_loop(0, idx_bits, cut_body, jnp.zeros((1, tq), I32))

        def drop_body(t, _):
            start = pl.multiple_of(t * ks, ks)
            sc = key_scr[pl.ds(start, ks), :]
            kidx = start + lax.broadcasted_iota(I32, (ks, tq), 0)
            key_scr[pl.ds(start, ks), :] = jnp.where((sc == tau) & (kidx > cut), -jnp.inf, sc)
            return 0

        lax.fori_loop(0, n_trips, drop_body, 0)

    m_scr[...] = jnp.full(m_scr.shape, NEG_BIG, F32)
    l_scr[...] = jnp.zeros(l_scr.shape, F32)
    acc_scr[...] = jnp.zeros(acc_scr.shape, F32)
    c2 = scale * LOG2E

    def logits(c):
        kd = kd_ref[0, pl.ds(pl.multiple_of(c * kc, kc), kc), :]
        return lax.dot_general(kd, qa_scr[...], NT_DIMS, preferred_element_type=F32)

    def weighted_values(c):
        vt = vt_ref[0, :, pl.ds(pl.multiple_of(c * kc, kc), kc)]
        acc_scr[...] = alpha_scr[...] * acc_scr[...] + jnp.dot(
            vt, p_scr[...], preferred_element_type=F32)

    s_scr[...] = logits(0)
    p_scr[...] = jnp.zeros(p_scr.shape, BF16)
    alpha_scr[...] = jnp.ones(alpha_scr.shape, F32)

    def att_body(c, _):
        weighted_values(jnp.maximum(c - 1, 0))
        nxt = logits(jnp.minimum(c + 1, n_chunks - 1))
        sel = key_scr[pl.ds(pl.multiple_of(c * kc, kc), kc), :] >= tau_scr[...]
        s = s_scr[...]
        s = jnp.concatenate([jnp.where(sel, s[:, h * tq:(h + 1) * tq], NEG_BIG)
                             for h in range(n_dsa)], axis=1)
        m_old = m_scr[...]
        m_new = jnp.maximum(m_old, jnp.max(s, axis=0, keepdims=True))
        alpha = jnp.exp2((m_old - m_new) * c2)
        p = jnp.exp2((s - m_new) * c2)
        l_scr[...] = alpha * l_scr[...] + jnp.sum(p, axis=0, keepdims=True)
        m_scr[...] = m_new
        alpha_scr[...] = alpha
        p_scr[...] = p.astype(BF16)
        s_scr[...] = nxt
        return 0

    lax.fori_loop(0, n_chunks, att_body, 0)
    weighted_values(n_chunks - 1)

    out_t = acc_scr[...] / l_scr[...]
    for h in range(n_dsa):
        o_ref[0, :, h * HEAD_DIM:(h + 1) * HEAD_DIM] = out_t[:, h * tq:(h + 1) * tq].T


def _dsa(proj3, v_t, w_t, batch, seq, n_dsa, scale, tq=256, kc=256):
    width = n_dsa * HEAD_DIM
    topk = min(TOPK_MAX, seq // 4)
    idx_bits = int(np.ceil(np.log2(seq)))
    kern = functools.partial(_dsa_kernel, scale=scale, topk=topk, kc=kc, idx_bits=idx_bits)
    assert width == 1024 and IDX_HEADS * IDX_DIM == 1024
    misc = 5 * 1024 // LANES
    return pl.pallas_call(
        kern,
        out_shape=jax.ShapeDtypeStruct((batch, seq, width), F32),
        grid=(batch, seq // tq),
        in_specs=[
            pl.BlockSpec((1, tq, width), lambda b, i: (b, i, 3)),
            pl.BlockSpec((1, tq, IDX_HEADS * IDX_DIM), lambda b, i: (b, i, 4)),
            pl.BlockSpec((1, seq, HEAD_DIM), lambda b, i: (b, 0, misc)),
            pl.BlockSpec((1, HEAD_DIM, seq), lambda b, i: (b, 0, 0)),
            pl.BlockSpec((1, seq, LANES), lambda b, i: (b, 0, misc + 2)),
            pl.BlockSpec((1, IDX_HEADS, tq), lambda b, i: (b, 0, i)),
        ],
        out_specs=pl.BlockSpec((1, tq, width), lambda b, i: (b, i, 0)),
        scratch_shapes=[
            pltpu.VMEM((IDX_HEADS * tq, LANES), BF16),
            pltpu.VMEM((n_dsa * tq, HEAD_DIM), BF16),
            pltpu.VMEM((seq + kc, tq), F32),
            pltpu.VMEM((HEAD_DIM, n_dsa * tq), F32),
            pltpu.VMEM((1, n_dsa * tq), F32),
            pltpu.VMEM((1, n_dsa * tq), F32),
            pltpu.VMEM((1, tq), F32),
            pltpu.VMEM((kc, n_dsa * tq), F32),
            pltpu.VMEM((kc, n_dsa * tq), BF16),
            pltpu.VMEM((1, n_dsa * tq), F32),
        ],
        compiler_params=_cparams(("arbitrary", "arbitrary")),
        name="dsa",
    )(proj3, proj3, proj3, v_t, proj3, w_t)


def _rms(v, g):
    return v * lax.rsqrt(jnp.mean(v * v, axis=-1, keepdims=True) + EPS) * g


def _outp_kernel(osb_ref, ods_ref, x_ref, mod_ref, gsb_ref, gds_ref, g2_ref, w_ref,
                 x1_ref, h2_ref):
    half = osb_ref.shape[1]
    a = _rms(osb_ref[...], gsb_ref[...]).astype(BF16)
    b = _rms(ods_ref[...], gds_ref[...]).astype(BF16)
    y = (jnp.dot(a, w_ref[0:half, :], preferred_element_type=F32)
         + jnp.dot(b, w_ref[half:, :], preferred_element_type=F32))
    x1 = x_ref[...] + mod_ref[0, 2:3, :] * y
    x1_ref[...] = x1
    h2 = _rms(x1, g2_ref[...]) * (1.0 + mod_ref[0, 4:5, :]) + mod_ref[0, 3:4, :]
    h2_ref[...] = h2.astype(BF16)


def _outp(o_sb, o_ds, x2d, mod3, g_sb, g_ds, g2, w_out, seq, tm=512):
    rows, d = x2d.shape
    half = o_sb.shape[1]
    tiles_per_batch = seq // tm
    return pl.pallas_call(
        _outp_kernel,
        out_shape=(jax.ShapeDtypeStruct((rows, d), F32), jax.ShapeDtypeStruct((rows, d), BF16)),
        grid=(rows // tm,),
        in_specs=[
            pl.BlockSpec((tm, half), lambda i: (i, 0)),
            pl.BlockSpec((tm, half), lambda i: (i, 0)),
            pl.BlockSpec((tm, d), lambda i: (i, 0)),
            pl.BlockSpec((1, 6, d), lambda i: (i // tiles_per_batch, 0, 0)),
            pl.BlockSpec((1, half), lambda i: (0, 0)),
            pl.BlockSpec((1, half), lambda i: (0, 0)),
            pl.BlockSpec((1, d), lambda i: (0, 0)),
            pl.BlockSpec((2 * half, d), lambda i: (0, 0)),
        ],
        out_specs=(pl.BlockSpec((tm, d), lambda i: (i, 0)), pl.BlockSpec((tm, d), lambda i: (i, 0))),
        compiler_params=_cparams(("arbitrary",)),
        name="outp",
    )(o_sb, o_ds, x2d, mod3, g_sb, g_ds, g2, w_out)


def _ffn_kernel(h_ref, halo_ref, x1_ref, mod_ref, wg_ref, wv_ref, cwg_ref, cwv_ref,
                cbg_ref, cbv_ref, wd_ref, gf_ref, o_ref, *, tiles_per_batch, final_norm):
    i = pl.program_id(0)
    j = pl.program_id(1)
    tm = h_ref.shape[0]
    pad = halo_ref.shape[0]
    h = h_ref[...]
    halo = jnp.where(i % tiles_per_batch == 0, jnp.zeros_like(halo_ref[...]), halo_ref[...])

    row8 = lax.broadcasted_iota(I32, (8, 1), 0)

    def conv(w_ref, cw_ref, cb_ref, cols):
        w = w_ref[:, cols]
        u = jnp.dot(h, w, preferred_element_type=F32)
        tail = jnp.dot(halo, w, preferred_element_type=F32)[pad - 8:, :]
        uc = cb_ref[:, cols] + u * cw_ref[CONV_WIDTH - 1:CONV_WIDTH, cols]
        for tap in range(CONV_WIDTH - 1):
            back = CONV_WIDTH - 1 - tap
            rolled = pltpu.roll(u, back, axis=0)
            head = jnp.where(row8 < back, pltpu.roll(tail, back, axis=0), rolled[:8, :])
            shifted = jnp.concatenate([head, rolled[8:, :]], axis=0)
            uc = uc + shifted * cw_ref[tap:tap + 1, cols]
        return uc

    tn = wg_ref.shape[1]
    part = jnp.where(j > 0, o_ref[...], 0.0)
    for cols in (slice(0, tn // 2), slice(tn // 2, tn)):
        gate = conv(wg_ref, cwg_ref, cbg_ref, cols)
        val = conv(wv_ref, cwv_ref, cbv_ref, cols)
        a = (gate * jax.nn.sigmoid(gate) * val).astype(BF16)
        part = part + jnp.dot(a, wd_ref[cols, :], preferred_element_type=F32)
    o_ref[...] = part

    @pl.when(j == pl.num_programs(1) - 1)
    def _():
        x2 = x1_ref[...] + mod_ref[0, 5:6, :] * o_ref[...]
        o_ref[...] = _rms(x2, gf_ref[...]) if final_norm else x2


def _ffn(h2, x1, mod3, w_up_p, cw_g, cw_v, cb_g, cb_v, w_dn, g_f, seq, final_norm,
         tm=1024, tn=512):
    rows, d = x1.shape
    dff = w_up_p.shape[1] // 2
    tiles_per_batch = seq // tm
    pad = 16
    kern = functools.partial(_ffn_kernel, tiles_per_batch=tiles_per_batch, final_norm=final_norm)
    return pl.pallas_call(
        kern,
        out_shape=jax.ShapeDtypeStruct((rows, d), F32),
        grid=(rows // tm, dff // tn),
        in_specs=[
            pl.BlockSpec((tm, d), lambda i, j: (i, 0), pipeline_mode=pl.Buffered(1)),
            pl.BlockSpec((pad, d), lambda i, j: (jnp.maximum(i * (tm // pad) - 1, 0), 0)),
            pl.BlockSpec((tm, d), lambda i, j: (i, 0), pipeline_mode=pl.Buffered(1)),
            pl.BlockSpec((1, 6, d), lambda i, j: (i // tiles_per_batch, 0, 0)),
            pl.BlockSpec((d, tn), lambda i, j: (0, j)),
            pl.BlockSpec((d, tn), lambda i, j: (0, j + dff // tn)),
            pl.BlockSpec((CONV_WIDTH, tn), lambda i, j: (0, j)),
            pl.BlockSpec((CONV_WIDTH, tn), lambda i, j: (0, j)),
            pl.BlockSpec((1, tn), lambda i, j: (0, j)),
            pl.BlockSpec((1, tn), lambda i, j: (0, j)),
            pl.BlockSpec((tn, d), lambda i, j: (j, 0)),
            pl.BlockSpec((1, d), lambda i, j: (0, 0)),
        ],
        out_specs=pl.BlockSpec((tm, d), lambda i, j: (i, 0), pipeline_mode=pl.Buffered(1)),
        compiler_params=_cparams(("arbitrary", "arbitrary")),
        name="ffn",
    )(h2, h2, x1, mod3, w_up_p, w_up_p, cw_g, cw_v, cb_g, cb_v, w_dn, g_f)


def _pad_cols(a, n):
    return jnp.pad(a, ((0, 0), (0, n - a.shape[1])))


def _regroup_rows_kernel(tbl_ref, x_ref, o_ref, *, n_plain):
    j = pl.program_id(0)

    @pl.when(j < n_plain)
    def _():
        o_ref[...] = x_ref[...].astype(o_ref.dtype)

    @pl.when(j == n_plain)
    def _():
        k = x_ref[0:IDX_DIM, :].astype(o_ref.dtype)
        o_ref[0:IDX_DIM, :] = k
        o_ref[IDX_DIM:2 * IDX_DIM, :] = k

    @pl.when(j == n_plain + 1)
    def _():
        o_ref[...] = jnp.zeros(o_ref.shape, o_ref.dtype)
        o_ref[0:IDX_HEADS, :] = x_ref[IDX_DIM:IDX_DIM + IDX_HEADS, :].astype(o_ref.dtype)


def _regroup_w_in_t(w_t, table):
    d = w_t.shape[1]
    n = len(table)
    return pl.pallas_call(
        functools.partial(_regroup_rows_kernel, n_plain=n - 2),
        out_shape=jax.ShapeDtypeStruct((n * LANES, d), BF16),
        grid_spec=pltpu.PrefetchScalarGridSpec(
            num_scalar_prefetch=1,
            grid=(n,),
            in_specs=[pl.BlockSpec((LANES, d), lambda j, t: (t[j], 0))],
            out_specs=pl.BlockSpec((LANES, d), lambda j, t: (j, 0)),
        ),
        compiler_params=_cparams(("arbitrary",)),
        name="prep_w_in",
    )(jnp.asarray(table, I32), w_t)


def _pad_halves_kernel(x_ref, o_ref):
    valid = x_ref.shape[1]
    o_ref[:, 0:valid] = x_ref[...].astype(o_ref.dtype)
    o_ref[:, valid:] = jnp.zeros((o_ref.shape[0], o_ref.shape[1] - valid), o_ref.dtype)


def _pad_halves_bf16(w, half, half_pad, tr=256):
    d = w.shape[0]
    return pl.pallas_call(
        _pad_halves_kernel,
        out_shape=jax.ShapeDtypeStruct((d, 2 * half_pad), BF16),
        grid=(d // tr, 2),
        in_specs=[pl.BlockSpec((tr, half), lambda i, g: (i, g))],
        out_specs=pl.BlockSpec((tr, half_pad), lambda i, g: (i, g)),
        compiler_params=_cparams(("arbitrary", "arbitrary")),
        name="prep_w_up",
    )(w)


def _pad_rows_kernel(x_ref, o_ref, *, valid):
    tr = x_ref.shape[0]
    row = pl.program_id(0) * tr + lax.broadcasted_iota(I32, (tr, 1), 0)
    o_ref[...] = jnp.where(row < valid, x_ref[...], 0.0).astype(o_ref.dtype)


def _pad_rows_bf16(w, n_rows, tr=512):
    valid, d = w.shape
    return pl.pallas_call(
        functools.partial(_pad_rows_kernel, valid=valid),
        out_shape=jax.ShapeDtypeStruct((n_rows, d), BF16),
        grid=(n_rows // tr,),
        in_specs=[pl.BlockSpec((tr, d), lambda j: (j, 0))],
        out_specs=pl.BlockSpec((tr, d), lambda j: (j, 0)),
        compiler_params=_cparams(("arbitrary",)),
        name="prep_w_down",
    )(w)


def _w_in_table(d_model):
    blk = d_model // 2 // LANES
    q_sb, k_sb, v_sb, q_ds = (list(range(g * blk, (g + 1) * blk)) for g in range(4))
    k_ds, v_ds = 4 * blk, 4 * blk + 1
    q_ix = list(range(4 * blk + 2, 4 * blk + 2 + IDX_HEADS * IDX_DIM // LANES))
    tail = q_ix[-1] + 1
    return q_sb + k_sb + v_sb + q_ds + q_ix + [k_ds, v_ds, tail, tail]


def kernel(x, c, positions, w_ada, b_ada, norm1_g, w_in, sb_norm_g, dsa_norm_g, w_out, norm2_g,
           w_up, conv_w, conv_b, w_down, final_norm_g):
    batch, seq, d = x.shape
    depth = w_ada.shape[0]
    n_sb = (d // HEAD_DIM) // 2
    n_dsa = d // HEAD_DIM - n_sb
    d_ff = w_down.shape[1]
    ff_pad = -(-d_ff // 512) * 512
    idx_scale = (IDX_HEADS ** -0.5) * (IDX_DIM ** -0.5)
    scale = HEAD_DIM ** -0.5
    rows = batch * seq

    half128 = HEAD_DIM // ROPE_FRACTION // 2
    half64 = IDX_DIM // ROPE_FRACTION // 2
    f128 = ROPE_THETA ** (-jnp.arange(half128, dtype=F32) / half128)
    f64 = ROPE_THETA ** (-jnp.arange(half64, dtype=F32) / half64)
    invf = jnp.zeros((8, LANES), F32)
    invf = invf.at[0].set(jnp.tile(f128, LANES // half128)).at[1].set(jnp.tile(f64, LANES // half64))

    tri_np = np.zeros((2 * SUB, 2 * SUB), np.float32)
    later = (np.arange(SUB)[:, None] > np.arange(SUB)[None, :]).astype(np.float32)
    tri_np[:SUB, :SUB] = later
    tri_np[SUB:, :SUB] = later
    tri_np[:, SUB:] = 1.0
    tri = jnp.asarray(tri_np, BF16)

    pos_col = positions.reshape(rows, 1)
    c_pad = jnp.pad(c, ((0, 16 - batch), (0, 0)))
    x2d = x.reshape(rows, d)

    for l in range(depth):
        mod = _ada(c_pad, w_ada[l], b_ada[l][None, :])[:batch]
        mod3 = mod.reshape(batch, 6, d)

        w_in_p = _regroup_w_in_t(jnp.swapaxes(w_in[l], 0, 1), _w_in_table(d))
        proj, v_t, w_t = _proj(x2d, mod3, norm1_g[l][None, :], pos_col, invf, w_in_p,
                               batch, seq, idx_scale)
        proj3 = proj.reshape(batch, seq, proj.shape[1])

        o_sb = _sb(proj3, tri, batch, seq, n_sb, scale)
        o_ds = _dsa(proj3, v_t, w_t, batch, seq, n_dsa, scale)

        x1, h2 = _outp(o_sb.reshape(rows, -1), o_ds.reshape(rows, -1), x2d, mod3,
                       sb_norm_g[l][None, :], dsa_norm_g[l][None, :], norm2_g[l][None, :],
                       w_out[l].astype(BF16), seq)

        w_up_p = _pad_halves_bf16(w_up[l], d_ff, ff_pad)
        cw_g = _pad_cols(conv_w[l][:, :d_ff], ff_pad)
        cw_v = _pad_cols(conv_w[l][:, d_ff:], ff_pad)
        cb_g = _pad_cols(conv_b[l][None, :d_ff], ff_pad)
        cb_v = _pad_cols(conv_b[l][None, d_ff:], ff_pad)
        w_dn = _pad_rows_bf16(w_down[l], ff_pad)
        x2d = _ffn(h2, x1, mod3, w_up_p, cw_g, cw_v, cb_g, cb_v, w_dn,
                   final_norm_g[None, :], seq, final_norm=(l == depth - 1))
    return x2d.reshape(batch, seq, d)
```

```python
import functools

import jax
import jax.numpy as jnp
import numpy as np
from jax import lax
from jax.experimental import pallas as pl
from jax.experimental.pallas import tpu as pltpu

F32 = jnp.float32
BF16 = jnp.bfloat16
I32 = jnp.int32

HEAD_DIM = 128
IDX_HEADS = 16
IDX_DIM = 64
TOPK_MAX = 256
ROPE_THETA = 500000.0
ROPE_FRACTION = 4
CONV_WIDTH = 3
EPS = 1e-6
NEG_BIG = -1e30
INT_MIN = -(2 ** 31)
LOG2E = 1.4426950408889634

LANES = 128
VMEM_LIMIT = 56 * 1024 * 1024

NT_DIMS = (((1,), (1,)), ((), ()))


def _cparams(sem):
    return pltpu.CompilerParams(dimension_semantics=sem, vmem_limit_bytes=VMEM_LIMIT)


def _ada_kernel(c_ref, w_ref, b_ref, o_ref):
    c = c_ref[...]
    s = (c * jax.nn.sigmoid(c)).astype(BF16)
    o_ref[...] = jnp.dot(s, w_ref[...].astype(BF16), preferred_element_type=F32) + b_ref[...]


def _ada(c_pad, w_ada, b_ada, tn=1024):
    m, d = c_pad.shape
    n = w_ada.shape[1]
    return pl.pallas_call(
        _ada_kernel,
        out_shape=jax.ShapeDtypeStruct((m, n), F32),
        grid=(n // tn,),
        in_specs=[
            pl.BlockSpec((m, d), lambda j: (0, 0)),
            pl.BlockSpec((d, tn), lambda j: (0, j)),
            pl.BlockSpec((1, tn), lambda j: (0, j)),
        ],
        out_specs=pl.BlockSpec((m, tn), lambda j: (0, j)),
        compiler_params=_cparams(("arbitrary",)),
        name="ada",
    )(c_pad, w_ada, b_ada)


def _rope(xb, c, sn, sp, shift):
    return (xb * c + pltpu.roll(xb, LANES - shift, axis=1) * sn
            + pltpu.roll(xb, shift, axis=1) * sp)


def _proj_kernel(x_ref, mod_ref, g_ref, pos_ref, invf_ref, w_ref,
                 o_ref, vt_ref, wt_ref, h_scr, tab_scr, *, idx_scale):
    j = pl.program_id(1)
    tn = o_ref.shape[1]
    nsub = tn // LANES

    @pl.when(j == 0)
    def _():
        x = x_ref[...]
        ms = jnp.mean(x * x, axis=-1, keepdims=True)
        y = x * lax.rsqrt(ms + EPS) * g_ref[...]
        h = y * (1.0 + mod_ref[0, 1:2, :]) + mod_ref[0, 0:1, :]
        h_scr[...] = h.astype(BF16)
        pos = pos_ref[...].astype(F32)
        lane = lax.broadcasted_iota(I32, (1, LANES), 1)
        a128 = pos * invf_ref[0:1, :]
        c128, s128 = jnp.cos(a128), jnp.sin(a128)
        tab_scr[0] = jnp.where(lane < 32, c128, 1.0)
        tab_scr[1] = jnp.where(lane < 16, -s128, 0.0)
        tab_scr[2] = jnp.where((lane >= 16) & (lane < 32), s128, 0.0)
        l64 = lane & 63
        a64 = pos * invf_ref[1:2, :]
        c64, s64 = jnp.cos(a64), jnp.sin(a64)
        tab_scr[3] = jnp.where(l64 < 16, c64, 1.0)
        tab_scr[4] = jnp.where(l64 < 8, -s64, 0.0)
        tab_scr[5] = jnp.where((l64 >= 8) & (l64 < 16), s64, 0.0)

    acc = lax.dot_general(h_scr[...], w_ref[...], NT_DIMS, preferred_element_type=F32)

    def sub(s):
        return acc[:, s * LANES:(s + 1) * LANES]

    def rope128(xb):
        return _rope(xb, tab_scr[0], tab_scr[1], tab_scr[2], 16)

    def rope64(xb):
        return _rope(xb, tab_scr[3], tab_scr[4], tab_scr[5], 8)

    def store(s, val):
        o_ref[:, s * LANES:(s + 1) * LANES] = val.astype(BF16)

    @pl.when(j < 6)
    def _():
        o_ref[...] = acc.astype(BF16)

    @pl.when((j == 6) | (j == 7))
    def _():
        for s in range(nsub):
            store(s, rope128(sub(s)))

    @pl.when((j == 8) | (j == 9))
    def _():
        for s in range(nsub):
            store(s, rope64(sub(s)))

    @pl.when(j == 10)
    def _():
        store(0, rope128(sub(0)))
        v = sub(1)
        store(1, v)
        vt_ref[0] = v.T.astype(BF16)
        store(2, rope64(sub(2)))
        w = sub(3)
        store(3, w)
        wt_ref[0] = (w * idx_scale).T[0:IDX_HEADS, :]


def _proj(x2d, mod3, g1, pos_col, invf, w_in_p, batch, seq, idx_scale, tm=1024, tn=512):
    rows, d = x2d.shape
    ncols = w_in_p.shape[0]
    tiles_per_batch = seq // tm
    kern = functools.partial(_proj_kernel, idx_scale=idx_scale)
    return pl.pallas_call(
        kern,
        out_shape=(
            jax.ShapeDtypeStruct((rows, ncols), BF16),
            jax.ShapeDtypeStruct((batch, HEAD_DIM, seq), BF16),
            jax.ShapeDtypeStruct((batch, IDX_HEADS, seq), F32),
        ),
        grid=(rows // tm, ncols // tn),
        in_specs=[
            pl.BlockSpec((tm, d), lambda i, j: (i, 0)),
            pl.BlockSpec((1, 6, d), lambda i, j: (i // tiles_per_batch, 0, 0)),
            pl.BlockSpec((1, d), lambda i, j: (0, 0)),
            pl.BlockSpec((tm, 1), lambda i, j: (i, 0)),
            pl.BlockSpec((8, LANES), lambda i, j: (0, 0)),
            pl.BlockSpec((tn, d), lambda i, j: (j, 0)),
        ],
        out_specs=(
            pl.BlockSpec((tm, tn), lambda i, j: (i, j)),
            pl.BlockSpec((1, HEAD_DIM, tm),
                         lambda i, j: (i // tiles_per_batch, 0, i % tiles_per_batch)),
            pl.BlockSpec((1, IDX_HEADS, tm),
                         lambda i, j: (i // tiles_per_batch, 0, i % tiles_per_batch)),
        ),
        scratch_shapes=[
            pltpu.VMEM((tm, d), BF16),
            pltpu.VMEM((6, tm, LANES), F32),
        ],
        compiler_params=_cparams(("arbitrary", "arbitrary")),
        name="proj",
    )(x2d, mod3, g1, pos_col, invf, w_in_p)


SUB = 128


def _sb_kernel(q_ref, k_ref, v_ref, tri_ref, o_ref, carry_scr, acc_scr, *, scale, kc):
    i = pl.program_id(2)
    tq = q_ref.shape[1]
    nsub = kc // SUB
    n_diag = tq // kc
    tri = tri_ref[...]

    def chunk(j, diag, row0=0):
        rows = slice(row0, tq)
        start = pl.multiple_of(j * kc, kc)
        k = k_ref[0, pl.ds(start, kc), :]
        v = v_ref[0, pl.ds(start, kc), :]
        zn = lax.dot_general(q_ref[0, rows, :], k, NT_DIMS,
                             preferred_element_type=F32) * (-scale)
        sp = jnp.log(1.0 + jnp.exp2(jnp.abs(zn) * (-LOG2E)))
        lneg = jnp.minimum(zn, 0.0) - sp
        lpos = lneg - zn
        if diag:
            t_row = i * tq + row0 + lax.broadcasted_iota(I32, (tq - row0, kc), 0)
            strict = (start + lax.broadcasted_iota(I32, (tq - row0, kc), 1)) < t_row
            lneg = jnp.where(strict, lneg, 0.0)
        hi = lneg.astype(BF16)
        lo = (lneg - hi.astype(F32)).astype(BF16)
        carry = carry_scr[rows, :]
        parts = [None] * nsub
        for s in reversed(range(nsub)):
            cols = slice(s * SUB, (s + 1) * SUB)
            r = jnp.dot(jnp.concatenate([hi[:, cols], lo[:, cols]], axis=1), tri,
                        preferred_element_type=F32)
            parts[s] = lpos[:, cols] + r[:, :SUB] + carry
            carry = carry + r[:, SUB:]
        carry_scr[rows, :] = carry
        w = jnp.exp(jnp.concatenate(parts, axis=1))
        if diag:
            w = jnp.where(strict, w, 0.0)
        acc_scr[rows, :] += jnp.dot(w.astype(BF16), v, preferred_element_type=F32)

    carry_scr[...] = jnp.zeros(carry_scr.shape, F32)
    acc_scr[...] = jnp.zeros(acc_scr.shape, F32)
    first = i * n_diag
    for n in reversed(range(n_diag)):
        chunk(first + n, True, row0=n * kc)

    def body(n, _):
        for u in range(n_diag):
            chunk(first - 1 - n_diag * n - u, False)
        return 0

    lax.fori_loop(0, i, body, 0)
    o_ref[0] = acc_scr[...]


def _sb(proj3, tri, batch, seq, n_heads, scale, tq=1024, kc=256):
    kern = functools.partial(_sb_kernel, scale=scale, kc=kc)
    return pl.pallas_call(
        kern,
        out_shape=jax.ShapeDtypeStruct((batch, seq, n_heads * HEAD_DIM), F32),
        grid=(batch, n_heads, seq // tq),
        in_specs=[
            pl.BlockSpec((1, tq, HEAD_DIM), lambda b, h, i: (b, i, h)),
            pl.BlockSpec((1, seq, HEAD_DIM), lambda b, h, i: (b, 0, n_heads + h)),
            pl.BlockSpec((1, seq, HEAD_DIM), lambda b, h, i: (b, 0, 2 * n_heads + h)),
            pl.BlockSpec((2 * SUB, 2 * SUB), lambda b, h, i: (0, 0)),
        ],
        out_specs=pl.BlockSpec((1, tq, HEAD_DIM), lambda b, h, i: (b, i, h)),
        scratch_shapes=[pltpu.VMEM((tq, SUB), F32), pltpu.VMEM((tq, HEAD_DIM), F32)],
        compiler_params=_cparams(("arbitrary", "arbitrary", "arbitrary")),
        name="sb",
    )(proj3, proj3, proj3, tri)


def _dsa_kernel(qd_ref, qi_ref, kd_ref, vt_ref, ki_ref, wt_ref, o_ref,
                qm_scr, qa_scr, key_scr, acc_scr, m_scr, l_scr, tau_scr, s_scr, p_scr, alpha_scr,
                *, scale, topk, kc, idx_bits):
    i = pl.program_id(1)
    tq = qd_ref.shape[1]
    n_dsa = qd_ref.shape[2] // HEAD_DIM
    n_chunks = ((i + 1) * tq + kc - 1) // kc
    ks = 2 * kc
    n_trips = (n_chunks + 1) // 2
    t_row = i * tq + lax.broadcasted_iota(I32, (1, tq), 1)

    lane = lax.broadcasted_iota(I32, (tq, LANES), 1)
    for h in range(IDX_HEADS):
        blk = qi_ref[0, :, (h // 2) * LANES:(h // 2 + 1) * LANES]
        keep = (lane < IDX_DIM) if h % 2 == 0 else (lane >= IDX_DIM)
        qm_scr[h * tq:(h + 1) * tq, :] = jnp.where(keep, blk, jnp.zeros_like(blk))
    for h in range(n_dsa):
        qa_scr[h * tq:(h + 1) * tq, :] = qd_ref[0, :, h * HEAD_DIM:(h + 1) * HEAD_DIM]

    def score_chunk(c, masked):
        start = pl.multiple_of(c * kc, kc)
        kch = ki_ref[0, pl.ds(start, kc), :]
        r = lax.dot_general(kch, qm_scr[...], NT_DIMS, preferred_element_type=F32)
        sc = jnp.zeros((kc, tq), F32)
        for h in range(IDX_HEADS):
            sc = sc + wt_ref[0, h:h + 1, :] * jnp.maximum(r[:, h * tq:(h + 1) * tq], 0.0)
        if masked:
            kidx = start + lax.broadcasted_iota(I32, (kc, tq), 0)
            sc = jnp.where(kidx <= t_row, sc, -jnp.inf)
        key_scr[pl.ds(start, kc), :] = sc

    def score_body(c, _):
        score_chunk(c, False)
        return 0

    lax.fori_loop(0, n_chunks - 1, score_body, 0)
    score_chunk(n_chunks - 1, True)
    key_scr[pl.ds(pl.multiple_of(n_chunks * kc, kc), kc), :] = jnp.full((kc, tq), -jnp.inf, F32)

    def as_float(ordered):
        return pltpu.bitcast(jnp.where(ordered >= 0, ordered, ordered ^ 0x7FFFFFFF), F32)

    def count(pred):
        def body(t, cnt):
            start = pl.multiple_of(t * ks, ks)
            hit = jnp.where(pred(key_scr[pl.ds(start, ks), :], start), 1, 0)
            parts = [hit[r * 8:(r + 1) * 8, :] for r in range(ks // 8)]
            while len(parts) > 1:
                parts = [a + b for a, b in zip(parts[0::2], parts[1::2])]
            return cnt + parts[0]

        cnt8 = lax.fori_loop(0, n_trips, body, jnp.zeros((8, tq), I32))
        return jnp.sum(cnt8, axis=0, keepdims=True)

    def tau_body(b, ordered):
        cand = ordered + lax.shift_left(jnp.int32(1), 31 - b)
        cand_f = as_float(cand)
        return jnp.where(count(lambda sc, start: sc >= cand_f) >= topk, cand, ordered)

    ordered = lax.fori_loop(0, 32, tau_body, jnp.full((1, tq), INT_MIN, I32))
    tau = jnp.where(ordered == INT_MIN, jnp.finfo(F32).min, as_float(ordered))
    tau_scr[...] = tau
    cnt_ge = count(lambda sc, start: sc >= tau)

    @pl.when(jnp.max(cnt_ge) > topk)
    def _():
        cnt_gt = count(lambda sc, start: sc > tau)

        def cut_body(b, cut):
            cand = cut + lax.shift_left(jnp.int32(1), idx_bits - 1 - b)

            def pred(sc, start):
                kidx = start + lax.broadcasted_iota(I32, (ks, tq), 0)
                return (sc == tau) & (kidx < cand)

            return jnp.where(cnt_gt + count(pred) < topk, cand, cut)

        cut = lax.fori_loop(0, idx_bits, cut_body, jnp.zeros((1, tq), I32))

        def drop_body(t, _):
            start = pl.multiple_of(t * ks, ks)
            sc = key_scr[pl.ds(start, ks), :]
            kidx = start + lax.broadcasted_iota(I32, (ks, tq), 0)
            key_scr[pl.ds(start, ks), :] = jnp.where((sc == tau) & (kidx > cut), -jnp.inf, sc)
            return 0

        lax.fori_loop(0, n_trips, drop_body, 0)

    m_scr[...] = jnp.full(m_scr.shape, NEG_BIG, F32)
    l_scr[...] = jnp.zeros(l_scr.shape, F32)
    acc_scr[...] = jnp.zeros(acc_scr.shape, F32)
    c2 = scale * LOG2E

    def logits(c):
        kd = kd_ref[0, pl.ds(pl.multiple_of(c * kc, kc), kc), :]
        return lax.dot_general(kd, qa_scr[...], NT_DIMS, preferred_element_type=F32)

    def weighted_values(c):
        vt = vt_ref[0, :, pl.ds(pl.multiple_of(c * kc, kc), kc)]
        acc_scr[...] = alpha_scr[...] * acc_scr[...] + jnp.dot(
            vt, p_scr[...], preferred_element_type=F32)

    s_scr[...] = logits(0)
    p_scr[...] = jnp.zeros(p_scr.shape, BF16)
    alpha_scr[...] = jnp.ones(alpha_scr.shape, F32)

    def att_body(c, _):
        weighted_values(jnp.maximum(c - 1, 0))
        nxt = logits(jnp.minimum(c + 1, n_chunks - 1))
        sel = key_scr[pl.ds(pl.multiple_of(c * kc, kc), kc), :] >= tau_scr[...]
        s = s_scr[...]
        s = jnp.concatenate([jnp.where(sel, s[:, h * tq:(h + 1) * tq], NEG_BIG)
                             for h in range(n_dsa)], axis=1)
        m_old = m_scr[...]
        m_new = jnp.maximum(m_old, jnp.max(s, axis=0, keepdims=True))
        alpha = jnp.exp2((m_old - m_new) * c2)
        p = jnp.exp2((s - m_new) * c2)
        l_scr[...] = alpha * l_scr[...] + jnp.sum(p, axis=0, keepdims=True)
        m_scr[...] = m_new
        alpha_scr[...] = alpha
        p_scr[...] = p.astype(BF16)
        s_scr[...] = nxt
        return 0

    lax.fori_loop(0, n_chunks, att_body, 0)
    weighted_values(n_chunks - 1)

    out_t = acc_scr[...] / l_scr[...]
    for h in range(n_dsa):
        o_ref[0, :, h * HEAD_DIM:(h + 1) * HEAD_DIM] = out_t[:, h * tq:(h + 1) * tq].T


def _dsa(proj3, v_t, w_t, batch, seq, n_dsa, scale, tq=256, kc=256):
    width = n_dsa * HEAD_DIM
    topk = min(TOPK_MAX, seq // 4)
    idx_bits = int(np.ceil(np.log2(seq)))
    kern = functools.partial(_dsa_kernel, scale=scale, topk=topk, kc=kc, idx_bits=idx_bits)
    assert width == 1024 and IDX_HEADS * IDX_DIM == 1024
    misc = 5 * 1024 // LANES
    return pl.pallas_call(
        kern,
        out_shape=jax.ShapeDtypeStruct((batch, seq, width), F32),
        grid=(batch, seq // tq),
        in_specs=[
            pl.BlockSpec((1, tq, width), lambda b, i: (b, i, 3)),
            pl.BlockSpec((1, tq, IDX_HEADS * IDX_DIM), lambda b, i: (b, i, 4)),
            pl.BlockSpec((1, seq, HEAD_DIM), lambda b, i: (b, 0, misc)),
            pl.BlockSpec((1, HEAD_DIM, seq), lambda b, i: (b, 0, 0)),
            pl.BlockSpec((1, seq, LANES), lambda b, i: (b, 0, misc + 2)),
            pl.BlockSpec((1, IDX_HEADS, tq), lambda b, i: (b, 0, i)),
        ],
        out_specs=pl.BlockSpec((1, tq, width), lambda b, i: (b, i, 0)),
        scratch_shapes=[
            pltpu.VMEM((IDX_HEADS * tq, LANES), BF16),
            pltpu.VMEM((n_dsa * tq, HEAD_DIM), BF16),
            pltpu.VMEM((seq + kc, tq), F32),
            pltpu.VMEM((HEAD_DIM, n_dsa * tq), F32),
            pltpu.VMEM((1, n_dsa * tq), F32),
            pltpu.VMEM((1, n_dsa * tq), F32),
            pltpu.VMEM((1, tq), F32),
            pltpu.VMEM((kc, n_dsa * tq), F32),
            pltpu.VMEM((kc, n_dsa * tq), BF16),
            pltpu.VMEM((1, n_dsa * tq), F32),
        ],
        compiler_params=_cparams(("arbitrary", "arbitrary")),
        name="dsa",
    )(proj3, proj3, proj3, v_t, proj3, w_t)


def _rms(v, g):
    return v * lax.rsqrt(jnp.mean(v * v, axis=-1, keepdims=True) + EPS) * g


def _outp_kernel(osb_ref, ods_ref, x_ref, mod_ref, gsb_ref, gds_ref, g2_ref, w_ref,
                 x1_ref, h2_ref):
    half = osb_ref.shape[1]
    a = _rms(osb_ref[...], gsb_ref[...]).astype(BF16)
    b = _rms(ods_ref[...], gds_ref[...]).astype(BF16)
    y = (jnp.dot(a, w_ref[0:half, :], preferred_element_type=F32)
         + jnp.dot(b, w_ref[half:, :], preferred_element_type=F32))
    x1 = x_ref[...] + mod_ref[0, 2:3, :] * y
    x1_ref[...] = x1
    h2 = _rms(x1, g2_ref[...]) * (1.0 + mod_ref[0, 4:5, :]) + mod_ref[0, 3:4, :]
    h2_ref[...] = h2.astype(BF16)


def _outp(o_sb, o_ds, x2d, mod3, g_sb, g_ds, g2, w_out, seq, tm=512):
    rows, d = x2d.shape
    half = o_sb.shape[1]
    tiles_per_batch = seq // tm
    return pl.pallas_call(
        _outp_kernel,
        out_shape=(jax.ShapeDtypeStruct((rows, d), F32), jax.ShapeDtypeStruct((rows, d), BF16)),
        grid=(rows // tm,),
        in_specs=[
            pl.BlockSpec((tm, half), lambda i: (i, 0)),
            pl.BlockSpec((tm, half), lambda i: (i, 0)),
            pl.BlockSpec((tm, d), lambda i: (i, 0)),
            pl.BlockSpec((1, 6, d), lambda i: (i // tiles_per_batch, 0, 0)),
            pl.BlockSpec((1, half), lambda i: (0, 0)),
            pl.BlockSpec((1, half), lambda i: (0, 0)),
            pl.BlockSpec((1, d), lambda i: (0, 0)),
            pl.BlockSpec((2 * half, d), lambda i: (0, 0)),
        ],
        out_specs=(pl.BlockSpec((tm, d), lambda i: (i, 0)), pl.BlockSpec((tm, d), lambda i: (i, 0))),
        compiler_params=_cparams(("arbitrary",)),
        name="outp",
    )(o_sb, o_ds, x2d, mod3, g_sb, g_ds, g2, w_out)


def _ffn_kernel(h_ref, halo_ref, x1_ref, mod_ref, wg_ref, wv_ref, cwg_ref, cwv_ref,
                cbg_ref, cbv_ref, wd_ref, gf_ref, o_ref, *, tiles_per_batch, final_norm):
    i = pl.program_id(0)
    j = pl.program_id(1)
    tm = h_ref.shape[0]
    pad = halo_ref.shape[0]
    h = h_ref[...]
    halo = jnp.where(i % tiles_per_batch == 0, jnp.zeros_like(halo_ref[...]), halo_ref[...])

    row8 = lax.broadcasted_iota(I32, (8, 1), 0)

    def conv(w_ref, cw_ref, cb_ref, cols):
        w = w_ref[:, cols]
        u = jnp.dot(h, w, preferred_element_type=F32)
        tail = jnp.dot(halo, w, preferred_element_type=F32)[pad - 8:, :]
        uc = cb_ref[:, cols] + u * cw_ref[CONV_WIDTH - 1:CONV_WIDTH, cols]
        for tap in range(CONV_WIDTH - 1):
            back = CONV_WIDTH - 1 - tap
            rolled = pltpu.roll(u, back, axis=0)
            head = jnp.where(row8 < back, pltpu.roll(tail, back, axis=0), rolled[:8, :])
            shifted = jnp.concatenate([head, rolled[8:, :]], axis=0)
            uc = uc + shifted * cw_ref[tap:tap + 1, cols]
        return uc

    tn = wg_ref.shape[1]
    part = jnp.where(j > 0, o_ref[...], 0.0)
    for cols in (slice(0, tn // 2), slice(tn // 2, tn)):
        gate = conv(wg_ref, cwg_ref, cbg_ref, cols)
        val = conv(wv_ref, cwv_ref, cbv_ref, cols)
        a = (gate * jax.nn.sigmoid(gate) * val).astype(BF16)
        part = part + jnp.dot(a, wd_ref[cols, :], preferred_element_type=F32)
    o_ref[...] = part

    @pl.when(j == pl.num_programs(1) - 1)
    def _():
        x2 = x1_ref[...] + mod_ref[0, 5:6, :] * o_ref[...]
        o_ref[...] = _rms(x2, gf_ref[...]) if final_norm else x2


def _ffn(h2, x1, mod3, w_up_p, cw_g, cw_v, cb_g, cb_v, w_dn, g_f, seq, final_norm,
         tm=1024, tn=512):
    rows, d = x1.shape
    dff = w_up_p.shape[1] // 2
    tiles_per_batch = seq // tm
    pad = 16
    kern = functools.partial(_ffn_kernel, tiles_per_batch=tiles_per_batch, final_norm=final_norm)
    return pl.pallas_call(
        kern,
        out_shape=jax.ShapeDtypeStruct((rows, d), F32),
        grid=(rows // tm, dff // tn),
        in_specs=[
            pl.BlockSpec((tm, d), lambda i, j: (i, 0), pipeline_mode=pl.Buffered(1)),
            pl.BlockSpec((pad, d), lambda i, j: (jnp.maximum(i * (tm // pad) - 1, 0), 0)),
            pl.BlockSpec((tm, d), lambda i, j: (i, 0), pipeline_mode=pl.Buffered(1)),
            pl.BlockSpec((1, 6, d), lambda i, j: (i // tiles_per_batch, 0, 0)),
            pl.BlockSpec((d, tn), lambda i, j: (0, j)),
            pl.BlockSpec((d, tn), lambda i, j: (0, j + dff // tn)),
            pl.BlockSpec((CONV_WIDTH, tn), lambda i, j: (0, j)),
            pl.BlockSpec((CONV_WIDTH, tn), lambda i, j: (0, j)),
            pl.BlockSpec((1, tn), lambda i, j: (0, j)),
            pl.BlockSpec((1, tn), lambda i, j: (0, j)),
            pl.BlockSpec((tn, d), lambda i, j: (j, 0)),
            pl.BlockSpec((1, d), lambda i, j: (0, 0)),
        ],
        out_specs=pl.BlockSpec((tm, d), lambda i, j: (i, 0), pipeline_mode=pl.Buffered(1)),
        compiler_params=_cparams(("arbitrary", "arbitrary")),
        name="ffn",
    )(h2, h2, x1, mod3, w_up_p, w_up_p, cw_g, cw_v, cb_g, cb_v, w_dn, g_f)


def _pad_cols(a, n):
    return jnp.pad(a, ((0, 0), (0, n - a.shape[1])))


def _regroup_rows_kernel(tbl_ref, x_ref, o_ref, *, n_plain):
    j = pl.program_id(0)

    @pl.when(j < n_plain)
    def _():
        o_ref[...] = x_ref[...].astype(o_ref.dtype)

    @pl.when(j == n_plain)
    def _():
        k = x_ref[0:IDX_DIM, :].astype(o_ref.dtype)
        o_ref[0:IDX_DIM, :] = k
        o_ref[IDX_DIM:2 * IDX_DIM, :] = k

    @pl.when(j == n_plain + 1)
    def _():
        o_ref[...] = jnp.zeros(o_ref.shape, o_ref.dtype)
        o_ref[0:IDX_HEADS, :] = x_ref[IDX_DIM:IDX_DIM + IDX_HEADS, :].astype(o_ref.dtype)


def _regroup_w_in_t(w_t, table):
    d = w_t.shape[1]
    n = len(table)
    return pl.pallas_call(
        functools.partial(_regroup_rows_kernel, n_plain=n - 2),
        out_shape=jax.ShapeDtypeStruct((n * LANES, d), BF16),
        grid_spec=pltpu.PrefetchScalarGridSpec(
            num_scalar_prefetch=1,
            grid=(n,),
            in_specs=[pl.BlockSpec((LANES, d), lambda j, t: (t[j], 0))],
            out_specs=pl.BlockSpec((LANES, d), lambda j, t: (j, 0)),
        ),
        compiler_params=_cparams(("arbitrary",)),
        name="prep_w_in",
    )(jnp.asarray(table, I32), w_t)


def _pad_halves_kernel(x_ref, o_ref):
    valid = x_ref.shape[1]
    o_ref[:, 0:valid] = x_ref[...].astype(o_ref.dtype)
    o_ref[:, valid:] = jnp.zeros((o_ref.shape[0], o_ref.shape[1] - valid), o_ref.dtype)


def _pad_halves_bf16(w, half, half_pad, tr=256):
    d = w.shape[0]
    return pl.pallas_call(
        _pad_halves_kernel,
        out_shape=jax.ShapeDtypeStruct((d, 2 * half_pad), BF16),
        grid=(d // tr, 2),
        in_specs=[pl.BlockSpec((tr, half), lambda i, g: (i, g))],
        out_specs=pl.BlockSpec((tr, half_pad), lambda i, g: (i, g)),
        compiler_params=_cparams(("arbitrary", "arbitrary")),
        name="prep_w_up",
    )(w)


def _pad_rows_kernel(x_ref, o_ref, *, valid):
    tr = x_ref.shape[0]
    row = pl.program_id(0) * tr + lax.broadcasted_iota(I32, (tr, 1), 0)
    o_ref[...] = jnp.where(row < valid, x_ref[...], 0.0).astype(o_ref.dtype)


def _pad_rows_bf16(w, n_rows, tr=512):
    valid, d = w.shape
    return pl.pallas_call(
        functools.partial(_pad_rows_kernel, valid=valid),
        out_shape=jax.ShapeDtypeStruct((n_rows, d), BF16),
        grid=(n_rows // tr,),
        in_specs=[pl.BlockSpec((tr, d), lambda j: (j, 0))],
        out_specs=pl.BlockSpec((tr, d), lambda j: (j, 0)),
        compiler_params=_cparams(("arbitrary",)),
        name="prep_w_down",
    )(w)


def _w_in_table(d_model):
    blk = d_model // 2 // LANES
    q_sb, k_sb, v_sb, q_ds = (list(range(g * blk, (g + 1) * blk)) for g in range(4))
    k_ds, v_ds = 4 * blk, 4 * blk + 1
    q_ix = list(range(4 * blk + 2, 4 * blk + 2 + IDX_HEADS * IDX_DIM // LANES))
    tail = q_ix[-1] + 1
    return q_sb + k_sb + v_sb + q_ds + q_ix + [k_ds, v_ds, tail, tail]


def kernel(x, c, positions, w_ada, b_ada, norm1_g, w_in, sb_norm_g, dsa_norm_g, w_out, norm2_g,
           w_up, conv_w, conv_b, w_down, final_norm_g):
    batch, seq, d = x.shape
    depth = w_ada.shape[0]
    n_sb = (d // HEAD_DIM) // 2
    n_dsa = d // HEAD_DIM - n_sb
    d_ff = w_down.shape[1]
    ff_pad = -(-d_ff // 512) * 512
    idx_scale = (IDX_HEADS ** -0.5) * (IDX_DIM ** -0.5)
    scale = HEAD_DIM ** -0.5
    rows = batch * seq

    half128 = HEAD_DIM // ROPE_FRACTION // 2
    half64 = IDX_DIM // ROPE_FRACTION // 2
    f128 = ROPE_THETA ** (-jnp.arange(half128, dtype=F32) / half128)
    f64 = ROPE_THETA ** (-jnp.arange(half64, dtype=F32) / half64)
    invf = jnp.zeros((8, LANES), F32)
    invf = invf.at[0].set(jnp.tile(f128, LANES // half128)).at[1].set(jnp.tile(f64, LANES // half64))

    tri_np = np.zeros((2 * SUB, 2 * SUB), np.float32)
    later = (np.arange(SUB)[:, None] > np.arange(SUB)[None, :]).astype(np.float32)
    tri_np[:SUB, :SUB] = later
    tri_np[SUB:, :SUB] = later
    tri_np[:, SUB:] = 1.0
    tri = jnp.asarray(tri_np, BF16)

    pos_col = positions.reshape(rows, 1)
    c_pad = jnp.pad(c, ((0, 16 - batch), (0, 0)))
    x2d = x.reshape(rows, d)

    for l in range(depth):
        mod = _ada(c_pad, w_ada[l], b_ada[l][None, :])[:batch]
        mod3 = mod.reshape(batch, 6, d)

        w_in_p = _regroup_w_in_t(jnp.swapaxes(w_in[l], 0, 1), _w_in_table(d))
        proj, v_t, w_t = _proj(x2d, mod3, norm1_g[l][None, :], pos_col, invf, w_in_p,
                               batch, seq, idx_scale)
        proj3 = proj.reshape(batch, seq, proj.shape[1])

        o_sb = _sb(proj3, tri, batch, seq, n_sb, scale)
        o_ds = _dsa(proj3, v_t, w_t, batch, seq, n_dsa, scale)

        x1, h2 = _outp(o_sb.reshape(rows, -1), o_ds.reshape(rows, -1), x2d, mod3,
                       sb_norm_g[l][None, :], dsa_norm_g[l][None, :], norm2_g[l][None, :],
                       w_out[l].astype(BF16), seq)

        w_up_p = _pad_halves_bf16(w_up[l], d_ff, ff_pad)
        cw_g = _pad_cols(conv_w[l][:, :d_ff], ff_pad)
        cw_v = _pad_cols(conv_w[l][:, d_ff:], ff_pad)
        cb_g = _pad_cols(conv_b[l][None, :d_ff], ff_pad)
        cb_v = _pad_cols(conv_b[l][None, d_ff:], ff_pad)
        w_dn = _pad_rows_bf16(w_down[l], ff_pad)
        x2d = _ffn(h2, x1, mod3, w_up_p, cw_g, cw_v, cb_g, cb_v, w_dn,
                   final_norm_g[None, :], seq, final_norm=(l == depth - 1))
    return x2d.reshape(batch, seq, d)
```

```python
import functools

import jax
import jax.numpy as jnp
import numpy as np
from jax import lax
from jax.experimental import pallas as pl
from jax.experimental.pallas import tpu as pltpu

F32 = jnp.float32
BF16 = jnp.bfloat16
I32 = jnp.int32

HEAD_DIM = 128
IDX_HEADS = 16
IDX_DIM = 64
TOPK_MAX = 256
ROPE_THETA = 500000.0
ROPE_FRACTION = 4
CONV_WIDTH = 3
EPS = 1e-6
NEG_BIG = -1e30
INT_MIN = -(2 ** 31)
LOG2E = 1.4426950408889634

LANES = 128
VMEM_LIMIT = 56 * 1024 * 1024

NT_DIMS = (((1,), (1,)), ((), ()))


def _cparams(sem):
    return pltpu.CompilerParams(dimension_semantics=sem, vmem_limit_bytes=VMEM_LIMIT)


def _ada_kernel(c_ref, w_ref, b_ref, o_ref):
    c = c_ref[...]
    s = (c * jax.nn.sigmoid(c)).astype(BF16)
    o_ref[...] = jnp.dot(s, w_ref[...].astype(BF16), preferred_element_type=F32) + b_ref[...]


def _ada(c_pad, w_ada, b_ada, tn=1024):
    m, d = c_pad.shape
    n = w_ada.shape[1]
    return pl.pallas_call(
        _ada_kernel,
        out_shape=jax.ShapeDtypeStruct((m, n), F32),
        grid=(n // tn,),
        in_specs=[
            pl.BlockSpec((m, d), lambda j: (0, 0)),
            pl.BlockSpec((d, tn), lambda j: (0, j)),
            pl.BlockSpec((1, tn), lambda j: (0, j)),
        ],
        out_specs=pl.BlockSpec((m, tn), lambda j: (0, j)),
        compiler_params=_cparams(("arbitrary",)),
        name="ada",
    )(c_pad, w_ada, b_ada)


def _rope(xb, c, sn, sp, shift):
    return (xb * c + pltpu.roll(xb, LANES - shift, axis=1) * sn
            + pltpu.roll(xb, shift, axis=1) * sp)


def _proj_kernel(x_ref, mod_ref, g_ref, pos_ref, invf_ref, w_ref,
                 o_ref, vt_ref, wt_ref, h_scr, tab_scr, *, idx_scale):
    j = pl.program_id(1)
    tn = o_ref.shape[1]
    nsub = tn // LANES

    @pl.when(j == 0)
    def _():
        x = x_ref[...]
        ms = jnp.mean(x * x, axis=-1, keepdims=True)
        y = x * lax.rsqrt(ms + EPS) * g_ref[...]
        h = y * (1.0 + mod_ref[0, 1:2, :]) + mod_ref[0, 0:1, :]
        h_scr[...] = h.astype(BF16)
        pos = pos_ref[...].astype(F32)
        lane = lax.broadcasted_iota(I32, (1, LANES), 1)
        a128 = pos * invf_ref[0:1, :]
        c128, s128 = jnp.cos(a128), jnp.sin(a128)
        tab_scr[0] = jnp.where(lane < 32, c128, 1.0)
        tab_scr[1] = jnp.where(lane < 16, -s128, 0.0)
        tab_scr[2] = jnp.where((lane >= 16) & (lane < 32), s128, 0.0)
        l64 = lane & 63
        a64 = pos * invf_ref[1:2, :]
        c64, s64 = jnp.cos(a64), jnp.sin(a64)
        tab_scr[3] = jnp.where(l64 < 16, c64, 1.0)
        tab_scr[4] = jnp.where(l64 < 8, -s64, 0.0)
        tab_scr[5] = jnp.where((l64 >= 8) & (l64 < 16), s64, 0.0)

    acc = lax.dot_general(h_scr[...], w_ref[...], NT_DIMS, preferred_element_type=F32)

    def sub(s):
        return acc[:, s * LANES:(s + 1) * LANES]

    def rope128(xb):
        return _rope(xb, tab_scr[0], tab_scr[1], tab_scr[2], 16)

    def rope64(xb):
        return _rope(xb, tab_scr[3], tab_scr[4], tab_scr[5], 8)

    def store(s, val):
        o_ref[:, s * LANES:(s + 1) * LANES] = val.astype(BF16)

    @pl.when(j < 6)
    def _():
        o_ref[...] = acc.astype(BF16)

    @pl.when((j == 6) | (j == 7))
    def _():
        for s in range(nsub):
            store(s, rope128(sub(s)))

    @pl.when((j == 8) | (j == 9))
    def _():
        for s in range(nsub):
            store(s, rope64(sub(s)))

    @pl.when(j == 10)
    def _():
        store(0, rope128(sub(0)))
        v = sub(1)
        store(1, v)
        vt_ref[0] = v.T.astype(BF16)
        store(2, rope64(sub(2)))
        w = sub(3)
        store(3, w)
        wt_ref[0] = (w * idx_scale).T[0:IDX_HEADS, :]


def _proj(x2d, mod3, g1, pos_col, invf, w_in_p, batch, seq, idx_scale, tm=1024, tn=512):
    rows, d = x2d.shape
    ncols = w_in_p.shape[0]
    tiles_per_batch = seq // tm
    kern = functools.partial(_proj_kernel, idx_scale=idx_scale)
    return pl.pallas_call(
        kern,
        out_shape=(
            jax.ShapeDtypeStruct((rows, ncols), BF16),
            jax.ShapeDtypeStruct((batch, HEAD_DIM, seq), BF16),
            jax.ShapeDtypeStruct((batch, IDX_HEADS, seq), F32),
        ),
        grid=(rows // tm, ncols // tn),
        in_specs=[
            pl.BlockSpec((tm, d), lambda i, j: (i, 0)),
            pl.BlockSpec((1, 6, d), lambda i, j: (i // tiles_per_batch, 0, 0)),
            pl.BlockSpec((1, d), lambda i, j: (0, 0)),
            pl.BlockSpec((tm, 1), lambda i, j: (i, 0)),
            pl.BlockSpec((8, LANES), lambda i, j: (0, 0)),
            pl.BlockSpec((tn, d), lambda i, j: (j, 0)),
        ],
        out_specs=(
            pl.BlockSpec((tm, tn), lambda i, j: (i, j)),
            pl.BlockSpec((1, HEAD_DIM, tm),
                         lambda i, j: (i // tiles_per_batch, 0, i % tiles_per_batch)),
            pl.BlockSpec((1, IDX_HEADS, tm),
                         lambda i, j: (i // tiles_per_batch, 0, i % tiles_per_batch)),
        ),
        scratch_shapes=[
            pltpu.VMEM((tm, d), BF16),
            pltpu.VMEM((6, tm, LANES), F32),
        ],
        compiler_params=_cparams(("arbitrary", "arbitrary")),
        name="proj",
    )(x2d, mod3, g1, pos_col, invf, w_in_p)


SUB = 128


def _sb_kernel(q_ref, k_ref, v_ref, tri_ref, o_ref, carry_scr, acc_scr, *, scale, kc):
    i = pl.program_id(2)
    tq = q_ref.shape[1]
    nsub = kc // SUB
    n_diag = tq // kc
    tri = tri_ref[...]

    def chunk(j, diag, row0=0):
        rows = slice(row0, tq)
        start = pl.multiple_of(j * kc, kc)
        k = k_ref[0, pl.ds(start, kc), :]
        v = v_ref[0, pl.ds(start, kc), :]
        zn = lax.dot_general(q_ref[0, rows, :], k, NT_DIMS,
                             preferred_element_type=F32) * (-scale)
        sp = jnp.log(1.0 + jnp.exp2(jnp.abs(zn) * (-LOG2E)))
        lneg = jnp.minimum(zn, 0.0) - sp
        lpos = lneg - zn
        if diag:
            t_row = i * tq + row0 + lax.broadcasted_iota(I32, (tq - row0, kc), 0)
            strict = (start + lax.broadcasted_iota(I32, (tq - row0, kc), 1)) < t_row
            lneg = jnp.where(strict, lneg, 0.0)
        hi = lneg.astype(BF16)
        lo = (lneg - hi.astype(F32)).astype(BF16)
        carry = carry_scr[rows, :]
        parts = [None] * nsub
        for s in reversed(range(nsub)):
            cols = slice(s * SUB, (s + 1) * SUB)
            r = jnp.dot(jnp.concatenate([hi[:, cols], lo[:, cols]], axis=1), tri,
                        preferred_element_type=F32)
            parts[s] = lpos[:, cols] + r[:, :SUB] + carry
            carry = carry + r[:, SUB:]
        carry_scr[rows, :] = carry
        w = jnp.exp(jnp.concatenate(parts, axis=1))
        if diag:
            w = jnp.where(strict, w, 0.0)
        acc_scr[rows, :] += jnp.dot(w.astype(BF16), v, preferred_element_type=F32)

    carry_scr[...] = jnp.zeros(carry_scr.shape, F32)
    acc_scr[...] = jnp.zeros(acc_scr.shape, F32)
    first = i * n_diag
    for n in reversed(range(n_diag)):
        chunk(first + n, True, row0=n * kc)

    def body(n, _):
        for u in range(n_diag):
            chunk(first - 1 - n_diag * n - u, False)
        return 0

    lax.fori_loop(0, i, body, 0)
    o_ref[0] = acc_scr[...]


def _sb(proj3, tri, batch, seq, n_heads, scale, tq=2048, kc=256):
    kern = functools.partial(_sb_kernel, scale=scale, kc=kc)
    return pl.pallas_call(
        kern,
        out_shape=jax.ShapeDtypeStruct((batch, seq, n_heads * HEAD_DIM), F32),
        grid=(batch, n_heads, seq // tq),
        in_specs=[
            pl.BlockSpec((1, tq, HEAD_DIM), lambda b, h, i: (b, i, h)),
            pl.BlockSpec((1, seq, HEAD_DIM), lambda b, h, i: (b, 0, n_heads + h)),
            pl.BlockSpec((1, seq, HEAD_DIM), lambda b, h, i: (b, 0, 2 * n_heads + h)),
            pl.BlockSpec((2 * SUB, 2 * SUB), lambda b, h, i: (0, 0)),
        ],
        out_specs=pl.BlockSpec((1, tq, HEAD_DIM), lambda b, h, i: (b, i, h)),
        scratch_shapes=[pltpu.VMEM((tq, SUB), F32), pltpu.VMEM((tq, HEAD_DIM), F32)],
        compiler_params=_cparams(("arbitrary", "arbitrary", "arbitrary")),
        name="sb",
    )(proj3, proj3, proj3, tri)


def _dsa_kernel(qd_ref, qi_ref, kd_ref, vt_ref, ki_ref, wt_ref, o_ref,
                qm_scr, qa_scr, key_scr, acc_scr, m_scr, l_scr, tau_scr, s_scr, p_scr, alpha_scr,
                *, scale, topk, kc, idx_bits):
    i = pl.program_id(1)
    tq = qd_ref.shape[1]
    n_dsa = qd_ref.shape[2] // HEAD_DIM
    n_chunks = ((i + 1) * tq + kc - 1) // kc
    ks = 2 * kc
    n_trips = (n_chunks + 1) // 2
    t_row = i * tq + lax.broadcasted_iota(I32, (1, tq), 1)

    lane = lax.broadcasted_iota(I32, (tq, LANES), 1)
    for h in range(IDX_HEADS):
        blk = qi_ref[0, :, (h // 2) * LANES:(h // 2 + 1) * LANES]
        keep = (lane < IDX_DIM) if h % 2 == 0 else (lane >= IDX_DIM)
        qm_scr[h * tq:(h + 1) * tq, :] = jnp.where(keep, blk, jnp.zeros_like(blk))
    for h in range(n_dsa):
        qa_scr[h * tq:(h + 1) * tq, :] = qd_ref[0, :, h * HEAD_DIM:(h + 1) * HEAD_DIM]

    def score_chunk(c, masked):
        start = pl.multiple_of(c * kc, kc)
        kch = ki_ref[0, pl.ds(start, kc), :]
        r = lax.dot_general(kch, qm_scr[...], NT_DIMS, preferred_element_type=F32)
        sc = jnp.zeros((kc, tq), F32)
        for h in range(IDX_HEADS):
            sc = sc + wt_ref[0, h:h + 1, :] * jnp.maximum(r[:, h * tq:(h + 1) * tq], 0.0)
        if masked:
            kidx = start + lax.broadcasted_iota(I32, (kc, tq), 0)
            sc = jnp.where(kidx <= t_row, sc, -jnp.inf)
        key_scr[pl.ds(start, kc), :] = sc

    def score_body(c, _):
        score_chunk(c, False)
        return 0

    lax.fori_loop(0, n_chunks - 1, score_body, 0)
    score_chunk(n_chunks - 1, True)
    key_scr[pl.ds(pl.multiple_of(n_chunks * kc, kc), kc), :] = jnp.full((kc, tq), -jnp.inf, F32)

    def as_float(ordered):
        return pltpu.bitcast(jnp.where(ordered >= 0, ordered, ordered ^ 0x7FFFFFFF), F32)

    def count(pred):
        def body(t, cnt):
            start = pl.multiple_of(t * ks, ks)
            hit = jnp.where(pred(key_scr[pl.ds(start, ks), :], start), 1, 0)
            parts = [hit[r * 8:(r + 1) * 8, :] for r in range(ks // 8)]
            while len(parts) > 1:
                parts = [a + b for a, b in zip(parts[0::2], parts[1::2])]
            return cnt + parts[0]

        cnt8 = lax.fori_loop(0, n_trips, body, jnp.zeros((8, tq), I32))
        return jnp.sum(cnt8, axis=0, keepdims=True)

    def tau_body(b, ordered):
        cand = ordered + lax.shift_left(jnp.int32(1), 31 - b)
        cand_f = as_float(cand)
        return jnp.where(count(lambda sc, start: sc >= cand_f) >= topk, cand, ordered)

    ordered = lax.fori_loop(0, 32, tau_body, jnp.full((1, tq), INT_MIN, I32))
    tau = jnp.where(ordered == INT_MIN, jnp.finfo(F32).min, as_float(ordered))
    tau_scr[...] = tau
    cnt_ge = count(lambda sc, start: sc >= tau)

    @pl.when(jnp.max(cnt_ge) > topk)
    def _():
        cnt_gt = count(lambda sc, start: sc > tau)

        def cut_body(b, cut):
            cand = cut + lax.shift_left(jnp.int32(1), idx_bits - 1 - b)

            def pred(sc, start):
                kidx = start + lax.broadcasted_iota(I32, (ks, tq), 0)
                return (sc == tau) & (kidx < cand)

            return jnp.where(cnt_gt + count(pred) < topk, cand, cut)

        cut = lax.fori_loop(0, idx_bits, cut_body, jnp.zeros((1, tq), I32))

        def drop_body(t, _):
            start = pl.multiple_of(t * ks, ks)
            sc = key_scr[pl.ds(start, ks), :]
            kidx = start + lax.broadcasted_iota(I32, (ks, tq), 0)
            key_scr[pl.ds(start, ks), :] = jnp.where((sc == tau) & (kidx > cut), -jnp.inf, sc)
            return 0

        lax.fori_loop(0, n_trips, drop_body, 0)

    m_scr[...] = jnp.full(m_scr.shape, NEG_BIG, F32)
    l_scr[...] = jnp.zeros(l_scr.shape, F32)
    acc_scr[...] = jnp.zeros(acc_scr.shape, F32)
    c2 = scale * LOG2E

    def logits(c):
        kd = kd_ref[0, pl.ds(pl.multiple_of(c * kc, kc), kc), :]
        return lax.dot_general(kd, qa_scr[...], NT_DIMS, preferred_element_type=F32)

    def weighted_values(c):
        vt = vt_ref[0, :, pl.ds(pl.multiple_of(c * kc, kc), kc)]
        acc_scr[...] = alpha_scr[...] * acc_scr[...] + jnp.dot(
            vt, p_scr[...], preferred_element_type=F32)

    s_scr[...] = logits(0)
    p_scr[...] = jnp.zeros(p_scr.shape, BF16)
    alpha_scr[...] = jnp.ones(alpha_scr.shape, F32)

    def att_body(c, _):
        weighted_values(jnp.maximum(c - 1, 0))
        nxt = logits(jnp.minimum(c + 1, n_chunks - 1))
        sel = key_scr[pl.ds(pl.multiple_of(c * kc, kc), kc), :] >= tau_scr[...]
        s = s_scr[...]
        s = jnp.concatenate([jnp.where(sel, s[:, h * tq:(h + 1) * tq], NEG_BIG)
                             for h in range(n_dsa)], axis=1)
        m_old = m_scr[...]
        m_new = jnp.maximum(m_old, jnp.max(s, axis=0, keepdims=True))
        alpha = jnp.exp2((m_old - m_new) * c2)
        p = jnp.exp2((s - m_new) * c2)
        l_scr[...] = alpha * l_scr[...] + jnp.sum(p, axis=0, keepdims=True)
        m_scr[...] = m_new
        alpha_scr[...] = alpha
        p_scr[...] = p.astype(BF16)
        s_scr[...] = nxt
        return 0

    lax.fori_loop(0, n_chunks, att_body, 0)
    weighted_values(n_chunks - 1)

    out_t = acc_scr[...] / l_scr[...]
    for h in range(n_dsa):
        o_ref[0, :, h * HEAD_DIM:(h + 1) * HEAD_DIM] = out_t[:, h * tq:(h + 1) * tq].T


def _dsa(proj3, v_t, w_t, batch, seq, n_dsa, scale, tq=256, kc=256):
    width = n_dsa * HEAD_DIM
    topk = min(TOPK_MAX, seq // 4)
    idx_bits = int(np.ceil(np.log2(seq)))
    kern = functools.partial(_dsa_kernel, scale=scale, topk=topk, kc=kc, idx_bits=idx_bits)
    assert width == 1024 and IDX_HEADS * IDX_DIM == 1024
    misc = 5 * 1024 // LANES
    return pl.pallas_call(
        kern,
        out_shape=jax.ShapeDtypeStruct((batch, seq, width), F32),
        grid=(batch, seq // tq),
        in_specs=[
            pl.BlockSpec((1, tq, width), lambda b, i: (b, i, 3)),
            pl.BlockSpec((1, tq, IDX_HEADS * IDX_DIM), lambda b, i: (b, i, 4)),
            pl.BlockSpec((1, seq, HEAD_DIM), lambda b, i: (b, 0, misc)),
            pl.BlockSpec((1, HEAD_DIM, seq), lambda b, i: (b, 0, 0)),
            pl.BlockSpec((1, seq, LANES), lambda b, i: (b, 0, misc + 2)),
            pl.BlockSpec((1, IDX_HEADS, tq), lambda b, i: (b, 0, i)),
        ],
        out_specs=pl.BlockSpec((1, tq, width), lambda b, i: (b, i, 0)),
        scratch_shapes=[
            pltpu.VMEM((IDX_HEADS * tq, LANES), BF16),
            pltpu.VMEM((n_dsa * tq, HEAD_DIM), BF16),
            pltpu.VMEM((seq + kc, tq), F32),
            pltpu.VMEM((HEAD_DIM, n_dsa * tq), F32),
            pltpu.VMEM((1, n_dsa * tq), F32),
            pltpu.VMEM((1, n_dsa * tq), F32),
            pltpu.VMEM((1, tq), F32),
            pltpu.VMEM((kc, n_dsa * tq), F32),
            pltpu.VMEM((kc, n_dsa * tq), BF16),
            pltpu.VMEM((1, n_dsa * tq), F32),
        ],
        compiler_params=_cparams(("arbitrary", "arbitrary")),
        name="dsa",
    )(proj3, proj3, proj3, v_t, proj3, w_t)


def _rms(v, g):
    return v * lax.rsqrt(jnp.mean(v * v, axis=-1, keepdims=True) + EPS) * g


def _outp_kernel(osb_ref, ods_ref, x_ref, mod_ref, gsb_ref, gds_ref, g2_ref, w_ref,
                 x1_ref, h2_ref):
    half = osb_ref.shape[1]
    a = _rms(osb_ref[...], gsb_ref[...]).astype(BF16)
    b = _rms(ods_ref[...], gds_ref[...]).astype(BF16)
    y = (jnp.dot(a, w_ref[0:half, :], preferred_element_type=F32)
         + jnp.dot(b, w_ref[half:, :], preferred_element_type=F32))
    x1 = x_ref[...] + mod_ref[0, 2:3, :] * y
    x1_ref[...] = x1
    h2 = _rms(x1, g2_ref[...]) * (1.0 + mod_ref[0, 4:5, :]) + mod_ref[0, 3:4, :]
    h2_ref[...] = h2.astype(BF16)


def _outp(o_sb, o_ds, x2d, mod3, g_sb, g_ds, g2, w_out, seq, tm=512):
    rows, d = x2d.shape
    half = o_sb.shape[1]
    tiles_per_batch = seq // tm
    return pl.pallas_call(
        _outp_kernel,
        out_shape=(jax.ShapeDtypeStruct((rows, d), F32), jax.ShapeDtypeStruct((rows, d), BF16)),
        grid=(rows // tm,),
        in_specs=[
            pl.BlockSpec((tm, half), lambda i: (i, 0)),
            pl.BlockSpec((tm, half), lambda i: (i, 0)),
            pl.BlockSpec((tm, d), lambda i: (i, 0)),
            pl.BlockSpec((1, 6, d), lambda i: (i // tiles_per_batch, 0, 0)),
            pl.BlockSpec((1, half), lambda i: (0, 0)),
            pl.BlockSpec((1, half), lambda i: (0, 0)),
            pl.BlockSpec((1, d), lambda i: (0, 0)),
            pl.BlockSpec((2 * half, d), lambda i: (0, 0)),
        ],
        out_specs=(pl.BlockSpec((tm, d), lambda i: (i, 0)), pl.BlockSpec((tm, d), lambda i: (i, 0))),
        compiler_params=_cparams(("arbitrary",)),
        name="outp",
    )(o_sb, o_ds, x2d, mod3, g_sb, g_ds, g2, w_out)


def _ffn_kernel(h_ref, halo_ref, x1_ref, mod_ref, wg_ref, wv_ref, cwg_ref, cwv_ref,
                cbg_ref, cbv_ref, wd_ref, gf_ref, o_ref, *, tiles_per_batch, final_norm):
    i = pl.program_id(0)
    j = pl.program_id(1)
    tm = h_ref.shape[0]
    pad = halo_ref.shape[0]
    h = h_ref[...]
    halo = jnp.where(i % tiles_per_batch == 0, jnp.zeros_like(halo_ref[...]), halo_ref[...])

    row8 = lax.broadcasted_iota(I32, (8, 1), 0)

    def conv(w_ref, cw_ref, cb_ref, cols):
        w = w_ref[:, cols]
        u = jnp.dot(h, w, preferred_element_type=F32)
        tail = jnp.dot(halo, w, preferred_element_type=F32)[pad - 8:, :]
        uc = cb_ref[:, cols] + u * cw_ref[CONV_WIDTH - 1:CONV_WIDTH, cols]
        for tap in range(CONV_WIDTH - 1):
            back = CONV_WIDTH - 1 - tap
            rolled = pltpu.roll(u, back, axis=0)
            head = jnp.where(row8 < back, pltpu.roll(tail, back, axis=0), rolled[:8, :])
            shifted = jnp.concatenate([head, rolled[8:, :]], axis=0)
            uc = uc + shifted * cw_ref[tap:tap + 1, cols]
        return uc

    tn = wg_ref.shape[1]
    part = jnp.where(j > 0, o_ref[...], 0.0)
    for cols in (slice(0, tn // 2), slice(tn // 2, tn)):
        gate = conv(wg_ref, cwg_ref, cbg_ref, cols)
        val = conv(wv_ref, cwv_ref, cbv_ref, cols)
        a = (gate * jax.nn.sigmoid(gate) * val).astype(BF16)
        part = part + jnp.dot(a, wd_ref[cols, :], preferred_element_type=F32)
    o_ref[...] = part

    @pl.when(j == pl.num_programs(1) - 1)
    def _():
        x2 = x1_ref[...] + mod_ref[0, 5:6, :] * o_ref[...]
        o_ref[...] = _rms(x2, gf_ref[...]) if final_norm else x2


def _ffn(h2, x1, mod3, w_up_p, cw_g, cw_v, cb_g, cb_v, w_dn, g_f, seq, final_norm,
         tm=1024, tn=512):
    rows, d = x1.shape
    dff = w_up_p.shape[1] // 2
    tiles_per_batch = seq // tm
    pad = 16
    kern = functools.partial(_ffn_kernel, tiles_per_batch=tiles_per_batch, final_norm=final_norm)
    return pl.pallas_call(
        kern,
        out_shape=jax.ShapeDtypeStruct((rows, d), F32),
        grid=(rows // tm, dff // tn),
        in_specs=[
            pl.BlockSpec((tm, d), lambda i, j: (i, 0), pipeline_mode=pl.Buffered(1)),
            pl.BlockSpec((pad, d), lambda i, j: (jnp.maximum(i * (tm // pad) - 1, 0), 0)),
            pl.BlockSpec((tm, d), lambda i, j: (i, 0), pipeline_mode=pl.Buffered(1)),
            pl.BlockSpec((1, 6, d), lambda i, j: (i // tiles_per_batch, 0, 0)),
            pl.BlockSpec((d, tn), lambda i, j: (0, j)),
            pl.BlockSpec((d, tn), lambda i, j: (0, j + dff // tn)),
            pl.BlockSpec((CONV_WIDTH, tn), lambda i, j: (0, j)),
            pl.BlockSpec((CONV_WIDTH, tn), lambda i, j: (0, j)),
            pl.BlockSpec((1, tn), lambda i, j: (0, j)),
            pl.BlockSpec((1, tn), lambda i, j: (0, j)),
            pl.BlockSpec((tn, d), lambda i, j: (j, 0)),
            pl.BlockSpec((1, d), lambda i, j: (0, 0)),
        ],
        out_specs=pl.BlockSpec((tm, d), lambda i, j: (i, 0), pipeline_mode=pl.Buffered(1)),
        compiler_params=_cparams(("arbitrary", "arbitrary")),
        name="ffn",
    )(h2, h2, x1, mod3, w_up_p, w_up_p, cw_g, cw_v, cb_g, cb_v, w_dn, g_f)


def _pad_cols(a, n):
    return jnp.pad(a, ((0, 0), (0, n - a.shape[1])))


def _regroup_rows_kernel(tbl_ref, x_ref, o_ref, *, n_plain):
    j = pl.program_id(0)

    @pl.when(j < n_plain)
    def _():
        o_ref[...] = x_ref[...].astype(o_ref.dtype)

    @pl.when(j == n_plain)
    def _():
        k = x_ref[0:IDX_DIM, :].astype(o_ref.dtype)
        o_ref[0:IDX_DIM, :] = k
        o_ref[IDX_DIM:2 * IDX_DIM, :] = k

    @pl.when(j == n_plain + 1)
    def _():
        o_ref[...] = jnp.zeros(o_ref.shape, o_ref.dtype)
        o_ref[0:IDX_HEADS, :] = x_ref[IDX_DIM:IDX_DIM + IDX_HEADS, :].astype(o_ref.dtype)


def _regroup_w_in_t(w_t, table):
    d = w_t.shape[1]
    n = len(table)
    return pl.pallas_call(
        functools.partial(_regroup_rows_kernel, n_plain=n - 2),
        out_shape=jax.ShapeDtypeStruct((n * LANES, d), BF16),
        grid_spec=pltpu.PrefetchScalarGridSpec(
            num_scalar_prefetch=1,
            grid=(n,),
            in_specs=[pl.BlockSpec((LANES, d), lambda j, t: (t[j], 0))],
            out_specs=pl.BlockSpec((LANES, d), lambda j, t: (j, 0)),
        ),
        compiler_params=_cparams(("arbitrary",)),
        name="prep_w_in",
    )(jnp.asarray(table, I32), w_t)


def _pad_halves_kernel(x_ref, o_ref):
    valid = x_ref.shape[1]
    o_ref[:, 0:valid] = x_ref[...].astype(o_ref.dtype)
    o_ref[:, valid:] = jnp.zeros((o_ref.shape[0], o_ref.shape[1] - valid), o_ref.dtype)


def _pad_halves_bf16(w, half, half_pad, tr=256):
    d = w.shape[0]
    return pl.pallas_call(
        _pad_halves_kernel,
        out_shape=jax.ShapeDtypeStruct((d, 2 * half_pad), BF16),
        grid=(d // tr, 2),
        in_specs=[pl.BlockSpec((tr, half), lambda i, g: (i, g))],
        out_specs=pl.BlockSpec((tr, half_pad), lambda i, g: (i, g)),
        compiler_params=_cparams(("arbitrary", "arbitrary")),
        name="prep_w_up",
    )(w)


def _pad_rows_kernel(x_ref, o_ref, *, valid):
    tr = x_ref.shape[0]
    row = pl.program_id(0) * tr + lax.broadcasted_iota(I32, (tr, 1), 0)
    o_ref[...] = jnp.where(row < valid, x_ref[...], 0.0).astype(o_ref.dtype)


def _pad_rows_bf16(w, n_rows, tr=512):
    valid, d = w.shape
    return pl.pallas_call(
        functools.partial(_pad_rows_kernel, valid=valid),
        out_shape=jax.ShapeDtypeStruct((n_rows, d), BF16),
        grid=(n_rows // tr,),
        in_specs=[pl.BlockSpec((tr, d), lambda j: (j, 0))],
        out_specs=pl.BlockSpec((tr, d), lambda j: (j, 0)),
        compiler_params=_cparams(("arbitrary",)),
        name="prep_w_down",
    )(w)


def _w_in_table(d_model):
    blk = d_model // 2 // LANES
    q_sb, k_sb, v_sb, q_ds = (list(range(g * blk, (g + 1) * blk)) for g in range(4))
    k_ds, v_ds = 4 * blk, 4 * blk + 1
    q_ix = list(range(4 * blk + 2, 4 * blk + 2 + IDX_HEADS * IDX_DIM // LANES))
    tail = q_ix[-1] + 1
    return q_sb + k_sb + v_sb + q_ds + q_ix + [k_ds, v_ds, tail, tail]


def kernel(x, c, positions, w_ada, b_ada, norm1_g, w_in, sb_norm_g, dsa_norm_g, w_out, norm2_g,
           w_up, conv_w, conv_b, w_down, final_norm_g):
    batch, seq, d = x.shape
    depth = w_ada.shape[0]
    n_sb = (d // HEAD_DIM) // 2
    n_dsa = d // HEAD_DIM - n_sb
    d_ff = w_down.shape[1]
    ff_pad = -(-d_ff // 512) * 512
    idx_scale = (IDX_HEADS ** -0.5) * (IDX_DIM ** -0.5)
    scale = HEAD_DIM ** -0.5
    rows = batch * seq

    half128 = HEAD_DIM // ROPE_FRACTION // 2
    half64 = IDX_DIM // ROPE_FRACTION // 2
    f128 = ROPE_THETA ** (-jnp.arange(half128, dtype=F32) / half128)
    f64 = ROPE_THETA ** (-jnp.arange(half64, dtype=F32) / half64)
    invf = jnp.zeros((8, LANES), F32)
    invf = invf.at[0].set(jnp.tile(f128, LANES // half128)).at[1].set(jnp.tile(f64, LANES // half64))

    tri_np = np.zeros((2 * SUB, 2 * SUB), np.float32)
    later = (np.arange(SUB)[:, None] > np.arange(SUB)[None, :]).astype(np.float32)
    tri_np[:SUB, :SUB] = later
    tri_np[SUB:, :SUB] = later
    tri_np[:, SUB:] = 1.0
    tri = jnp.asarray(tri_np, BF16)

    pos_col = positions.reshape(rows, 1)
    c_pad = jnp.pad(c, ((0, 16 - batch), (0, 0)))
    x2d = x.reshape(rows, d)

    for l in range(depth):
        mod = _ada(c_pad, w_ada[l], b_ada[l][None, :])[:batch]
        mod3 = mod.reshape(batch, 6, d)

        w_in_p = _regroup_w_in_t(jnp.swapaxes(w_in[l], 0, 1), _w_in_table(d))
        proj, v_t, w_t = _proj(x2d, mod3, norm1_g[l][None, :], pos_col, invf, w_in_p,
                               batch, seq, idx_scale)
        proj3 = proj.reshape(batch, seq, proj.shape[1])

        o_sb = _sb(proj3, tri, batch, seq, n_sb, scale)
        o_ds = _dsa(proj3, v_t, w_t, batch, seq, n_dsa, scale)

        x1, h2 = _outp(o_sb.reshape(rows, -1), o_ds.reshape(rows, -1), x2d, mod3,
                       sb_norm_g[l][None, :], dsa_norm_g[l][None, :], norm2_g[l][None, :],
                       w_out[l].astype(BF16), seq)

        w_up_p = _pad_halves_bf16(w_up[l], d_ff, ff_pad)
        cw_g = _pad_cols(conv_w[l][:, :d_ff], ff_pad)
        cw_v = _pad_cols(conv_w[l][:, d_ff:], ff_pad)
        cb_g = _pad_cols(conv_b[l][None, :d_ff], ff_pad)
        cb_v = _pad_cols(conv_b[l][None, d_ff:], ff_pad)
        w_dn = _pad_rows_bf16(w_down[l], ff_pad)
        x2d = _ffn(h2, x1, mod3, w_up_p, cw_g, cw_v, cb_g, cb_v, w_dn,
                   final_norm_g[None, :], seq, final_norm=(l == depth - 1))
    return x2d.reshape(batch, seq, d)
```

```python
import functools

import jax
import jax.numpy as jnp
import numpy as np
from jax import lax
from jax.experimental import pallas as pl
from jax.experimental.pallas import tpu as pltpu

F32 = jnp.float32
BF16 = jnp.bfloat16
I32 = jnp.int32

HEAD_DIM = 128
IDX_HEADS = 16
IDX_DIM = 64
TOPK_MAX = 256
ROPE_THETA = 500000.0
ROPE_FRACTION = 4
CONV_WIDTH = 3
EPS = 1e-6
NEG_BIG = -1e30
INT_MIN = -(2 ** 31)
LOG2E = 1.4426950408889634

LANES = 128
VMEM_LIMIT = 56 * 1024 * 1024

NT_DIMS = (((1,), (1,)), ((), ()))


def _cparams(sem):
    return pltpu.CompilerParams(dimension_semantics=sem, vmem_limit_bytes=VMEM_LIMIT)


def _ada_kernel(c_ref, w_ref, b_ref, o_ref):
    c = c_ref[...]
    s = (c * jax.nn.sigmoid(c)).astype(BF16)
    o_ref[...] = jnp.dot(s, w_ref[...].astype(BF16), preferred_element_type=F32) + b_ref[...]


def _ada(c_pad, w_ada, b_ada, tn=1024):
    m, d = c_pad.shape
    n = w_ada.shape[1]
    return pl.pallas_call(
        _ada_kernel,
        out_shape=jax.ShapeDtypeStruct((m, n), F32),
        grid=(n // tn,),
        in_specs=[
            pl.BlockSpec((m, d), lambda j: (0, 0)),
            pl.BlockSpec((d, tn), lambda j: (0, j)),
            pl.BlockSpec((1, tn), lambda j: (0, j)),
        ],
        out_specs=pl.BlockSpec((m, tn), lambda j: (0, j)),
        compiler_params=_cparams(("arbitrary",)),
        name="ada",
    )(c_pad, w_ada, b_ada)


def _rope(xb, c, sn, sp, shift):
    return (xb * c + pltpu.roll(xb, LANES - shift, axis=1) * sn
            + pltpu.roll(xb, shift, axis=1) * sp)


def _proj_kernel(x_ref, mod_ref, g_ref, pos_ref, invf_ref, w_ref,
                 o_ref, vt_ref, wt_ref, h_scr, tab_scr, *, idx_scale):
    j = pl.program_id(1)
    tn = o_ref.shape[1]
    nsub = tn // LANES

    @pl.when(j == 0)
    def _():
        x = x_ref[...]
        ms = jnp.mean(x * x, axis=-1, keepdims=True)
        y = x * lax.rsqrt(ms + EPS) * g_ref[...]
        h = y * (1.0 + mod_ref[0, 1:2, :]) + mod_ref[0, 0:1, :]
        h_scr[...] = h.astype(BF16)
        pos = pos_ref[...].astype(F32)
        lane = lax.broadcasted_iota(I32, (1, LANES), 1)
        inv = jnp.where(lane < 32, invf_ref[0:1, :],
                        jnp.where((lane >= 64) & (lane < 80), invf_ref[1:2, :], 0.0))
        ang = pos * inv
        c128, s128 = jnp.cos(ang), jnp.sin(ang)
        tab_scr[0] = jnp.where(lane < 32, c128, 1.0)
        tab_scr[1] = jnp.where(lane < 16, -s128, 0.0)
        tab_scr[2] = jnp.where((lane >= 16) & (lane < 32), s128, 0.0)
        l64 = lane & 63
        c64 = jnp.where(lane < 64, pltpu.roll(c128, 64, axis=1), c128)
        s64 = jnp.where(lane < 64, pltpu.roll(s128, 64, axis=1), s128)
        tab_scr[3] = jnp.where(l64 < 16, c64, 1.0)
        tab_scr[4] = jnp.where(l64 < 8, -s64, 0.0)
        tab_scr[5] = jnp.where((l64 >= 8) & (l64 < 16), s64, 0.0)

    acc = lax.dot_general(h_scr[...], w_ref[...], NT_DIMS, preferred_element_type=F32)

    def sub(s):
        return acc[:, s * LANES:(s + 1) * LANES]

    def rope128(xb):
        return _rope(xb, tab_scr[0], tab_scr[1], tab_scr[2], 16)

    def rope64(xb):
        return _rope(xb, tab_scr[3], tab_scr[4], tab_scr[5], 8)

    def store(s, val):
        o_ref[:, s * LANES:(s + 1) * LANES] = val.astype(BF16)

    @pl.when(j < 6)
    def _():
        o_ref[...] = acc.astype(BF16)

    @pl.when((j == 6) | (j == 7))
    def _():
        for s in range(nsub):
            store(s, rope128(sub(s)))

    @pl.when((j == 8) | (j == 9))
    def _():
        for s in range(nsub):
            store(s, rope64(sub(s)))

    @pl.when(j == 10)
    def _():
        store(0, rope128(sub(0)))
        v = sub(1)
        store(1, v)
        vt_ref[0] = v.T.astype(BF16)
        store(2, rope64(sub(2)))
        w = sub(3)
        store(3, w)
        wt_ref[0] = (w * idx_scale).T[0:IDX_HEADS, :]


def _proj(x2d, mod3, g1, pos_col, invf, w_in_p, batch, seq, idx_scale, tm=1024, tn=512):
    rows, d = x2d.shape
    ncols = w_in_p.shape[0]
    tiles_per_batch = seq // tm
    kern = functools.partial(_proj_kernel, idx_scale=idx_scale)
    return pl.pallas_call(
        kern,
        out_shape=(
            jax.ShapeDtypeStruct((rows, ncols), BF16),
            jax.ShapeDtypeStruct((batch, HEAD_DIM, seq), BF16),
            jax.ShapeDtypeStruct((batch, IDX_HEADS, seq), F32),
        ),
        grid=(rows // tm, ncols // tn),
        in_specs=[
            pl.BlockSpec((tm, d), lambda i, j: (i, 0)),
            pl.BlockSpec((1, 6, d), lambda i, j: (i // tiles_per_batch, 0, 0)),
            pl.BlockSpec((1, d), lambda i, j: (0, 0)),
            pl.BlockSpec((tm, 1), lambda i, j: (i, 0)),
            pl.BlockSpec((8, LANES), lambda i, j: (0, 0)),
            pl.BlockSpec((tn, d), lambda i, j: (j, 0)),
        ],
        out_specs=(
            pl.BlockSpec((tm, tn), lambda i, j: (i, j)),
            pl.BlockSpec((1, HEAD_DIM, tm),
                         lambda i, j: (i // tiles_per_batch, 0, i % tiles_per_batch)),
            pl.BlockSpec((1, IDX_HEADS, tm),
                         lambda i, j: (i // tiles_per_batch, 0, i % tiles_per_batch)),
        ),
        scratch_shapes=[
            pltpu.VMEM((tm, d), BF16),
            pltpu.VMEM((6, tm, LANES), F32),
        ],
        compiler_params=_cparams(("arbitrary", "arbitrary")),
        name="proj",
    )(x2d, mod3, g1, pos_col, invf, w_in_p)


SUB = 128


def _sb_kernel(q_ref, k_ref, v_ref, tri_ref, o_ref, carry_scr, acc_scr, *, scale, kc):
    i = pl.program_id(2)
    tq = q_ref.shape[1]
    nsub = kc // SUB
    n_diag = tq // kc
    tri = tri_ref[...]

    def chunk(j, diag, row0=0):
        rows = slice(row0, tq)
        start = pl.multiple_of(j * kc, kc)
        k = k_ref[0, pl.ds(start, kc), :]
        v = v_ref[0, pl.ds(start, kc), :]
        zn = lax.dot_general(q_ref[0, rows, :], k, NT_DIMS,
                             preferred_element_type=F32) * (-scale)
        sp = jnp.log(1.0 + jnp.exp2(jnp.abs(zn) * (-LOG2E)))
        lneg = jnp.minimum(zn, 0.0) - sp
        lpos = lneg - zn
        if diag:
            t_row = i * tq + row0 + lax.broadcasted_iota(I32, (tq - row0, kc), 0)
            strict = (start + lax.broadcasted_iota(I32, (tq - row0, kc), 1)) < t_row
            lneg = jnp.where(strict, lneg, 0.0)
        hi = lneg.astype(BF16)
        lo = (lneg - hi.astype(F32)).astype(BF16)
        carry = carry_scr[rows, :]
        parts = [None] * nsub
        for s in reversed(range(nsub)):
            cols = slice(s * SUB, (s + 1) * SUB)
            r = jnp.dot(jnp.concatenate([hi[:, cols], lo[:, cols]], axis=1), tri,
                        preferred_element_type=F32)
            parts[s] = lpos[:, cols] + r[:, :SUB] + carry
            carry = carry + r[:, SUB:]
        carry_scr[rows, :] = carry
        w = jnp.exp(jnp.concatenate(parts, axis=1))
        if diag:
            w = jnp.where(strict, w, 0.0)
        acc_scr[rows, :] += jnp.dot(w.astype(BF16), v, preferred_element_type=F32)

    carry_scr[...] = jnp.zeros(carry_scr.shape, F32)
    acc_scr[...] = jnp.zeros(acc_scr.shape, F32)
    first = i * n_diag
    for n in reversed(range(n_diag)):
        chunk(first + n, True, row0=n * kc)

    def body(n, _):
        for u in range(n_diag):
            chunk(first - 1 - n_diag * n - u, False)
        return 0

    lax.fori_loop(0, i, body, 0)
    o_ref[0] = acc_scr[...]


def _sb(proj3, tri, batch, seq, n_heads, scale, tq=2048, kc=256):
    kern = functools.partial(_sb_kernel, scale=scale, kc=kc)
    return pl.pallas_call(
        kern,
        out_shape=jax.ShapeDtypeStruct((batch, seq, n_heads * HEAD_DIM), F32),
        grid=(batch, n_heads, seq // tq),
        in_specs=[
            pl.BlockSpec((1, tq, HEAD_DIM), lambda b, h, i: (b, i, h)),
            pl.BlockSpec((1, seq, HEAD_DIM), lambda b, h, i: (b, 0, n_heads + h)),
            pl.BlockSpec((1, seq, HEAD_DIM), lambda b, h, i: (b, 0, 2 * n_heads + h)),
            pl.BlockSpec((2 * SUB, 2 * SUB), lambda b, h, i: (0, 0)),
        ],
        out_specs=pl.BlockSpec((1, tq, HEAD_DIM), lambda b, h, i: (b, i, h)),
        scratch_shapes=[pltpu.VMEM((tq, SUB), F32), pltpu.VMEM((tq, HEAD_DIM), F32)],
        compiler_params=_cparams(("arbitrary", "arbitrary", "arbitrary")),
        name="sb",
    )(proj3, proj3, proj3, tri)


def _dsa_kernel(qd_ref, qi_ref, kd_ref, vt_ref, ki_ref, wt_ref, o_ref,
                qm_scr, qa_scr, key_scr, acc_scr, m_scr, l_scr, tau_scr, s_scr, p_scr, alpha_scr,
                *, scale, topk, kc, idx_bits):
    i = pl.program_id(1)
    tq = qd_ref.shape[1]
    n_dsa = qd_ref.shape[2] // HEAD_DIM
    n_chunks = ((i + 1) * tq + kc - 1) // kc
    ks = 2 * kc
    n_trips = (n_chunks + 1) // 2
    t_row = i * tq + lax.broadcasted_iota(I32, (1, tq), 1)

    lane = lax.broadcasted_iota(I32, (tq, LANES), 1)
    for h in range(IDX_HEADS):
        blk = qi_ref[0, :, (h // 2) * LANES:(h // 2 + 1) * LANES]
        keep = (lane < IDX_DIM) if h % 2 == 0 else (lane >= IDX_DIM)
        qm_scr[h * tq:(h + 1) * tq, :] = jnp.where(keep, blk, jnp.zeros_like(blk))
    for h in range(n_dsa):
        qa_scr[h * tq:(h + 1) * tq, :] = qd_ref[0, :, h * HEAD_DIM:(h + 1) * HEAD_DIM]

    def score_chunk(c, masked):
        start = pl.multiple_of(c * kc, kc)
        kch = ki_ref[0, pl.ds(start, kc), :]
        r = lax.dot_general(kch, qm_scr[...], NT_DIMS, preferred_element_type=F32)
        sc = jnp.zeros((kc, tq), F32)
        for h in range(IDX_HEADS):
            sc = sc + wt_ref[0, h:h + 1, :] * jnp.maximum(r[:, h * tq:(h + 1) * tq], 0.0)
        if masked:
            kidx = start + lax.broadcasted_iota(I32, (kc, tq), 0)
            sc = jnp.where(kidx <= t_row, sc, -jnp.inf)
        key_scr[pl.ds(start, kc), :] = sc

    def score_body(c, _):
        score_chunk(c, False)
        return 0

    lax.fori_loop(0, n_chunks - 1, score_body, 0)
    score_chunk(n_chunks - 1, True)
    key_scr[pl.ds(pl.multiple_of(n_chunks * kc, kc), kc), :] = jnp.full((kc, tq), -jnp.inf, F32)

    def as_float(ordered):
        return pltpu.bitcast(jnp.where(ordered >= 0, ordered, ordered ^ 0x7FFFFFFF), F32)

    def count(pred):
        def body(t, cnt):
            start = pl.multiple_of(t * ks, ks)
            hit = jnp.where(pred(key_scr[pl.ds(start, ks), :], start), 1, 0)
            parts = [hit[r * 8:(r + 1) * 8, :] for r in range(ks // 8)]
            while len(parts) > 1:
                parts = [a + b for a, b in zip(parts[0::2], parts[1::2])]
            return cnt + parts[0]

        cnt8 = lax.fori_loop(0, n_trips, body, jnp.zeros((8, tq), I32))
        return jnp.sum(cnt8, axis=0, keepdims=True)

    def tau_body(b, ordered):
        cand = ordered + lax.shift_left(jnp.int32(1), 31 - b)
        cand_f = as_float(cand)
        return jnp.where(count(lambda sc, start: sc >= cand_f) >= topk, cand, ordered)

    ordered = lax.fori_loop(0, 32, tau_body, jnp.full((1, tq), INT_MIN, I32))
    tau = jnp.where(ordered == INT_MIN, jnp.finfo(F32).min, as_float(ordered))
    tau_scr[...] = tau
    cnt_ge = count(lambda sc, start: sc >= tau)

    @pl.when(jnp.max(cnt_ge) > topk)
    def _():
        cnt_gt = count(lambda sc, start: sc > tau)

        def cut_body(b, cut):
            cand = cut + lax.shift_left(jnp.int32(1), idx_bits - 1 - b)

            def pred(sc, start):
                kidx = start + lax.broadcasted_iota(I32, (ks, tq), 0)
                return (sc == tau) & (kidx < cand)

            return jnp.where(cnt_gt + count(pred) < topk, cand, cut)

        cut = lax.fori_loop(0, idx_bits, cut_body, jnp.zeros((1, tq), I32))

        def drop_body(t, _):
            start = pl.multiple_of(t * ks, ks)
            sc = key_scr[pl.ds(start, ks), :]
            kidx = start + lax.broadcasted_iota(I32, (ks, tq), 0)
            key_scr[pl.ds(start, ks), :] = jnp.where((sc == tau) & (kidx > cut), -jnp.inf, sc)
            return 0

        lax.fori_loop(0, n_trips, drop_body, 0)

    m_scr[...] = jnp.full(m_scr.shape, NEG_BIG, F32)
    l_scr[...] = jnp.zeros(l_scr.shape, F32)
    acc_scr[...] = jnp.zeros(acc_scr.shape, F32)
    c2 = scale * LOG2E

    def logits(c):
        kd = kd_ref[0, pl.ds(pl.multiple_of(c * kc, kc), kc), :]
        return lax.dot_general(kd, qa_scr[...], NT_DIMS, preferred_element_type=F32)

    def weighted_values(c):
        vt = vt_ref[0, :, pl.ds(pl.multiple_of(c * kc, kc), kc)]
        acc_scr[...] = alpha_scr[...] * acc_scr[...] + jnp.dot(
            vt, p_scr[...], preferred_element_type=F32)

    s_scr[...] = logits(0)
    p_scr[...] = jnp.zeros(p_scr.shape, BF16)
    alpha_scr[...] = jnp.ones(alpha_scr.shape, F32)

    def att_body(c, _):
        weighted_values(jnp.maximum(c - 1, 0))
        nxt = logits(jnp.minimum(c + 1, n_chunks - 1))
        sel = key_scr[pl.ds(pl.multiple_of(c * kc, kc), kc), :] >= tau_scr[...]
        s = s_scr[...]
        s = jnp.concatenate([jnp.where(sel, s[:, h * tq:(h + 1) * tq], NEG_BIG)
                             for h in range(n_dsa)], axis=1)
        m_old = m_scr[...]
        m_new = jnp.maximum(m_old, jnp.max(s, axis=0, keepdims=True))
        alpha = jnp.exp2((m_old - m_new) * c2)
        p = jnp.exp2((s - m_new) * c2)
        l_scr[...] = alpha * l_scr[...] + jnp.sum(p, axis=0, keepdims=True)
        m_scr[...] = m_new
        alpha_scr[...] = alpha
        p_scr[...] = p.astype(BF16)
        s_scr[...] = nxt
        return 0

    lax.fori_loop(0, n_chunks, att_body, 0)
    weighted_values(n_chunks - 1)

    out_t = acc_scr[...] / l_scr[...]
    for h in range(n_dsa):
        o_ref[0, :, h * HEAD_DIM:(h + 1) * HEAD_DIM] = out_t[:, h * tq:(h + 1) * tq].T


def _dsa(proj3, v_t, w_t, batch, seq, n_dsa, scale, tq=256, kc=256):
    width = n_dsa * HEAD_DIM
    topk = min(TOPK_MAX, seq // 4)
    idx_bits = int(np.ceil(np.log2(seq)))
    kern = functools.partial(_dsa_kernel, scale=scale, topk=topk, kc=kc, idx_bits=idx_bits)
    assert width == 1024 and IDX_HEADS * IDX_DIM == 1024
    misc = 5 * 1024 // LANES
    return pl.pallas_call(
        kern,
        out_shape=jax.ShapeDtypeStruct((batch, seq, width), F32),
        grid=(batch, seq // tq),
        in_specs=[
            pl.BlockSpec((1, tq, width), lambda b, i: (b, i, 3)),
            pl.BlockSpec((1, tq, IDX_HEADS * IDX_DIM), lambda b, i: (b, i, 4)),
            pl.BlockSpec((1, seq, HEAD_DIM), lambda b, i: (b, 0, misc)),
            pl.BlockSpec((1, HEAD_DIM, seq), lambda b, i: (b, 0, 0)),
            pl.BlockSpec((1, seq, LANES), lambda b, i: (b, 0, misc + 2)),
            pl.BlockSpec((1, IDX_HEADS, tq), lambda b, i: (b, 0, i)),
        ],
        out_specs=pl.BlockSpec((1, tq, width), lambda b, i: (b, i, 0)),
        scratch_shapes=[
            pltpu.VMEM((IDX_HEADS * tq, LANES), BF16),
            pltpu.VMEM((n_dsa * tq, HEAD_DIM), BF16),
            pltpu.VMEM((seq + kc, tq), F32),
            pltpu.VMEM((HEAD_DIM, n_dsa * tq), F32),
            pltpu.VMEM((1, n_dsa * tq), F32),
            pltpu.VMEM((1, n_dsa * tq), F32),
            pltpu.VMEM((1, tq), F32),
            pltpu.VMEM((kc, n_dsa * tq), F32),
            pltpu.VMEM((kc, n_dsa * tq), BF16),
            pltpu.VMEM((1, n_dsa * tq), F32),
        ],
        compiler_params=_cparams(("arbitrary", "arbitrary")),
        name="dsa",
    )(proj3, proj3, proj3, v_t, proj3, w_t)


def _rms(v, g):
    return v * lax.rsqrt(jnp.mean(v * v, axis=-1, keepdims=True) + EPS) * g


def _outp_kernel(osb_ref, ods_ref, x_ref, mod_ref, gsb_ref, gds_ref, g2_ref, w_ref,
                 x1_ref, h2_ref):
    half = osb_ref.shape[1]
    a = _rms(osb_ref[...], gsb_ref[...]).astype(BF16)
    b = _rms(ods_ref[...], gds_ref[...]).astype(BF16)
    y = (jnp.dot(a, w_ref[0:half, :], preferred_element_type=F32)
         + jnp.dot(b, w_ref[half:, :], preferred_element_type=F32))
    x1 = x_ref[...] + mod_ref[0, 2:3, :] * y
    x1_ref[...] = x1
    h2 = _rms(x1, g2_ref[...]) * (1.0 + mod_ref[0, 4:5, :]) + mod_ref[0, 3:4, :]
    h2_ref[...] = h2.astype(BF16)


def _outp(o_sb, o_ds, x2d, mod3, g_sb, g_ds, g2, w_out, seq, tm=512):
    rows, d = x2d.shape
    half = o_sb.shape[1]
    tiles_per_batch = seq // tm
    return pl.pallas_call(
        _outp_kernel,
        out_shape=(jax.ShapeDtypeStruct((rows, d), F32), jax.ShapeDtypeStruct((rows, d), BF16)),
        grid=(rows // tm,),
        in_specs=[
            pl.BlockSpec((tm, half), lambda i: (i, 0)),
            pl.BlockSpec((tm, half), lambda i: (i, 0)),
            pl.BlockSpec((tm, d), lambda i: (i, 0)),
            pl.BlockSpec((1, 6, d), lambda i: (i // tiles_per_batch, 0, 0)),
            pl.BlockSpec((1, half), lambda i: (0, 0)),
            pl.BlockSpec((1, half), lambda i: (0, 0)),
            pl.BlockSpec((1, d), lambda i: (0, 0)),
            pl.BlockSpec((2 * half, d), lambda i: (0, 0)),
        ],
        out_specs=(pl.BlockSpec((tm, d), lambda i: (i, 0)), pl.BlockSpec((tm, d), lambda i: (i, 0))),
        compiler_params=_cparams(("arbitrary",)),
        name="outp",
    )(o_sb, o_ds, x2d, mod3, g_sb, g_ds, g2, w_out)


def _ffn_kernel(h_ref, halo_ref, x1_ref, mod_ref, wg_ref, wv_ref, cwg_ref, cwv_ref,
                cbg_ref, cbv_ref, wd_ref, gf_ref, o_ref, *, tiles_per_batch, final_norm):
    i = pl.program_id(0)
    j = pl.program_id(1)
    tm = h_ref.shape[0]
    pad = halo_ref.shape[0]
    h = h_ref[...]
    halo = jnp.where(i % tiles_per_batch == 0, jnp.zeros_like(halo_ref[...]), halo_ref[...])

    row8 = lax.broadcasted_iota(I32, (8, 1), 0)

    def conv(w_ref, cw_ref, cb_ref, cols):
        w = w_ref[:, cols]
        u = jnp.dot(h, w, preferred_element_type=F32)
        tail = jnp.dot(halo, w, preferred_element_type=F32)[pad - 8:, :]
        uc = cb_ref[:, cols] + u * cw_ref[CONV_WIDTH - 1:CONV_WIDTH, cols]
        for tap in range(CONV_WIDTH - 1):
            back = CONV_WIDTH - 1 - tap
            rolled = pltpu.roll(u, back, axis=0)
            head = jnp.where(row8 < back, pltpu.roll(tail, back, axis=0), rolled[:8, :])
            shifted = jnp.concatenate([head, rolled[8:, :]], axis=0)
            uc = uc + shifted * cw_ref[tap:tap + 1, cols]
        return uc

    tn = wg_ref.shape[1]
    part = jnp.where(j > 0, o_ref[...], 0.0)
    for cols in (slice(0, tn // 2), slice(tn // 2, tn)):
        gate = conv(wg_ref, cwg_ref, cbg_ref, cols)
        val = conv(wv_ref, cwv_ref, cbv_ref, cols)
        a = (gate * jax.nn.sigmoid(gate) * val).astype(BF16)
        part = part + jnp.dot(a, wd_ref[cols, :], preferred_element_type=F32)
    o_ref[...] = part

    @pl.when(j == pl.num_programs(1) - 1)
    def _():
        x2 = x1_ref[...] + mod_ref[0, 5:6, :] * o_ref[...]
        o_ref[...] = _rms(x2, gf_ref[...]) if final_norm else x2


def _ffn(h2, x1, mod3, w_up_p, cw_g, cw_v, cb_g, cb_v, w_dn, g_f, seq, final_norm,
         tm=1024, tn=512):
    rows, d = x1.shape
    dff = w_up_p.shape[1] // 2
    tiles_per_batch = seq // tm
    pad = 16
    kern = functools.partial(_ffn_kernel, tiles_per_batch=tiles_per_batch, final_norm=final_norm)
    return pl.pallas_call(
        kern,
        out_shape=jax.ShapeDtypeStruct((rows, d), F32),
        grid=(rows // tm, dff // tn),
        in_specs=[
            pl.BlockSpec((tm, d), lambda i, j: (i, 0), pipeline_mode=pl.Buffered(1)),
            pl.BlockSpec((pad, d), lambda i, j: (jnp.maximum(i * (tm // pad) - 1, 0), 0)),
            pl.BlockSpec((tm, d), lambda i, j: (i, 0), pipeline_mode=pl.Buffered(1)),
            pl.BlockSpec((1, 6, d), lambda i, j: (i // tiles_per_batch, 0, 0)),
            pl.BlockSpec((d, tn), lambda i, j: (0, j)),
            pl.BlockSpec((d, tn), lambda i, j: (0, j + dff // tn)),
            pl.BlockSpec((CONV_WIDTH, tn), lambda i, j: (0, j)),
            pl.BlockSpec((CONV_WIDTH, tn), lambda i, j: (0, j)),
            pl.BlockSpec((1, tn), lambda i, j: (0, j)),
            pl.BlockSpec((1, tn), lambda i, j: (0, j)),
            pl.BlockSpec((tn, d), lambda i, j: (j, 0)),
            pl.BlockSpec((1, d), lambda i, j: (0, 0)),
        ],
        out_specs=pl.BlockSpec((tm, d), lambda i, j: (i, 0), pipeline_mode=pl.Buffered(1)),
        compiler_params=_cparams(("arbitrary", "arbitrary")),
        name="ffn",
    )(h2, h2, x1, mod3, w_up_p, w_up_p, cw_g, cw_v, cb_g, cb_v, w_dn, g_f)


def _pad_cols(a, n):
    return jnp.pad(a, ((0, 0), (0, n - a.shape[1])))


def _regroup_rows_kernel(tbl_ref, x_ref, o_ref, *, n_plain):
    j = pl.program_id(0)

    @pl.when(j < n_plain)
    def _():
        o_ref[...] = x_ref[...].astype(o_ref.dtype)

    @pl.when(j == n_plain)
    def _():
        k = x_ref[0:IDX_DIM, :].astype(o_ref.dtype)
        o_ref[0:IDX_DIM, :] = k
        o_ref[IDX_DIM:2 * IDX_DIM, :] = k

    @pl.when(j == n_plain + 1)
    def _():
        o_ref[...] = jnp.zeros(o_ref.shape, o_ref.dtype)
        o_ref[0:IDX_HEADS, :] = x_ref[IDX_DIM:IDX_DIM + IDX_HEADS, :].astype(o_ref.dtype)


def _regroup_w_in_t(w_t, table):
    d = w_t.shape[1]
    n = len(table)
    return pl.pallas_call(
        functools.partial(_regroup_rows_kernel, n_plain=n - 2),
        out_shape=jax.ShapeDtypeStruct((n * LANES, d), BF16),
        grid_spec=pltpu.PrefetchScalarGridSpec(
            num_scalar_prefetch=1,
            grid=(n,),
            in_specs=[pl.BlockSpec((LANES, d), lambda j, t: (t[j], 0))],
            out_specs=pl.BlockSpec((LANES, d), lambda j, t: (j, 0)),
        ),
        compiler_params=_cparams(("arbitrary",)),
        name="prep_w_in",
    )(jnp.asarray(table, I32), w_t)


def _pad_halves_kernel(x_ref, o_ref):
    valid = x_ref.shape[1]
    o_ref[:, 0:valid] = x_ref[...].astype(o_ref.dtype)
    o_ref[:, valid:] = jnp.zeros((o_ref.shape[0], o_ref.shape[1] - valid), o_ref.dtype)


def _pad_halves_bf16(w, half, half_pad, tr=256):
    d = w.shape[0]
    return pl.pallas_call(
        _pad_halves_kernel,
        out_shape=jax.ShapeDtypeStruct((d, 2 * half_pad), BF16),
        grid=(d // tr, 2),
        in_specs=[pl.BlockSpec((tr, half), lambda i, g: (i, g))],
        out_specs=pl.BlockSpec((tr, half_pad), lambda i, g: (i, g)),
        compiler_params=_cparams(("arbitrary", "arbitrary")),
        name="prep_w_up",
    )(w)


def _pad_rows_kernel(x_ref, o_ref, *, valid):
    tr = x_ref.shape[0]
    row = pl.program_id(0) * tr + lax.broadcasted_iota(I32, (tr, 1), 0)
    o_ref[...] = jnp.where(row < valid, x_ref[...], 0.0).astype(o_ref.dtype)


def _pad_rows_bf16(w, n_rows, tr=512):
    valid, d = w.shape
    return pl.pallas_call(
        functools.partial(_pad_rows_kernel, valid=valid),
        out_shape=jax.ShapeDtypeStruct((n_rows, d), BF16),
        grid=(n_rows // tr,),
        in_specs=[pl.BlockSpec((tr, d), lambda j: (j, 0))],
        out_specs=pl.BlockSpec((tr, d), lambda j: (j, 0)),
        compiler_params=_cparams(("arbitrary",)),
        name="prep_w_down",
    )(w)


def _w_in_table(d_model):
    blk = d_model // 2 // LANES
    q_sb, k_sb, v_sb, q_ds = (list(range(g * blk, (g + 1) * blk)) for g in range(4))
    k_ds, v_ds = 4 * blk, 4 * blk + 1
    q_ix = list(range(4 * blk + 2, 4 * blk + 2 + IDX_HEADS * IDX_DIM // LANES))
    tail = q_ix[-1] + 1
    return q_sb + k_sb + v_sb + q_ds + q_ix + [k_ds, v_ds, tail, tail]


def kernel(x, c, positions, w_ada, b_ada, norm1_g, w_in, sb_norm_g, dsa_norm_g, w_out, norm2_g,
           w_up, conv_w, conv_b, w_down, final_norm_g):
    batch, seq, d = x.shape
    depth = w_ada.shape[0]
    n_sb = (d // HEAD_DIM) // 2
    n_dsa = d // HEAD_DIM - n_sb
    d_ff = w_down.shape[1]
    ff_pad = -(-d_ff // 512) * 512
    idx_scale = (IDX_HEADS ** -0.5) * (IDX_DIM ** -0.5)
    scale = HEAD_DIM ** -0.5
    rows = batch * seq

    half128 = HEAD_DIM // ROPE_FRACTION // 2
    half64 = IDX_DIM // ROPE_FRACTION // 2
    f128 = ROPE_THETA ** (-jnp.arange(half128, dtype=F32) / half128)
    f64 = ROPE_THETA ** (-jnp.arange(half64, dtype=F32) / half64)
    invf = jnp.zeros((8, LANES), F32)
    invf = invf.at[0].set(jnp.tile(f128, LANES // half128)).at[1].set(jnp.tile(f64, LANES // half64))

    tri_np = np.zeros((2 * SUB, 2 * SUB), np.float32)
    later = (np.arange(SUB)[:, None] > np.arange(SUB)[None, :]).astype(np.float32)
    tri_np[:SUB, :SUB] = later
    tri_np[SUB:, :SUB] = later
    tri_np[:, SUB:] = 1.0
    tri = jnp.asarray(tri_np, BF16)

    pos_col = positions.reshape(rows, 1)
    c_pad = jnp.pad(c, ((0, 16 - batch), (0, 0)))
    x2d = x.reshape(rows, d)

    for l in range(depth):
        mod = _ada(c_pad, w_ada[l], b_ada[l][None, :])[:batch]
        mod3 = mod.reshape(batch, 6, d)

        w_in_p = _regroup_w_in_t(jnp.swapaxes(w_in[l], 0, 1), _w_in_table(d))
        proj, v_t, w_t = _proj(x2d, mod3, norm1_g[l][None, :], pos_col, invf, w_in_p,
                               batch, seq, idx_scale)
        proj3 = proj.reshape(batch, seq, proj.shape[1])

        o_sb = _sb(proj3, tri, batch, seq, n_sb, scale)
        o_ds = _dsa(proj3, v_t, w_t, batch, seq, n_dsa, scale)

        x1, h2 = _outp(o_sb.reshape(rows, -1), o_ds.reshape(rows, -1), x2d, mod3,
                       sb_norm_g[l][None, :], dsa_norm_g[l][None, :], norm2_g[l][None, :],
                       w_out[l].astype(BF16), seq)

        w_up_p = _pad_halves_bf16(w_up[l], d_ff, ff_pad)
        cw_g = _pad_cols(conv_w[l][:, :d_ff], ff_pad)
        cw_v = _pad_cols(conv_w[l][:, d_ff:], ff_pad)
        cb_g = _pad_cols(conv_b[l][None, :d_ff], ff_pad)
        cb_v = _pad_cols(conv_b[l][None, d_ff:], ff_pad)
        w_dn = _pad_rows_bf16(w_down[l], ff_pad)
        x2d = _ffn(h2, x1, mod3, w_up_p, cw_g, cw_v, cb_g, cb_v, w_dn,
                   final_norm_g[None, :], seq, final_norm=(l == depth - 1))
    return x2d.reshape(batch, seq, d)
```
